```python
import math
import jax, jax.numpy as jnp
from jax import lax
import numpy as np

D_MODEL = 1024
BATCH = 32
SEQ = 256
DEPTH = 4
DEC_BATCH = 8
DEC_SEQ = 2048
PAST_LEN = 512

GRID_W = 64
MLA_HEADS = 6
MLA_NOPE = 64
MLA_ROPE = 32
MLA_DV = 64
MLA_KV_RANK = 128
DIFF_HEADS = 6
DIFF_DH = 32
DIFF_DV = 2 * DIFF_DH
FOURIER_GROUPS = 4
FOURIER_GROUP_W = 64
FOURIER_WIDTH = FOURIER_GROUPS * FOURIER_GROUP_W
MIX_WIDTH = MLA_HEADS * MLA_DV + DIFF_HEADS * DIFF_DV + FOURIER_WIDTH
D_FF = 2816
CONV_W = 3
ROPE_BASE = 10000.0
Q_BLOCK = 128
EPS = 1e-6
MLA_SCALE = (MLA_NOPE + MLA_ROPE) ** -0.5
DIFF_SCALE = DIFF_DH ** -0.5
_MLA_Q_END = MLA_HEADS * (MLA_NOPE + MLA_ROPE)
_CKV_END = _MLA_Q_END + MLA_KV_RANK
_KROPE_END = _CKV_END + MLA_ROPE
_DQ_END = _KROPE_END + DIFF_HEADS * 2 * DIFF_DH
_DK_END = _DQ_END + DIFF_HEADS * 2 * DIFF_DH
_DV_END = _DK_END + DIFF_HEADS * DIFF_DV
IN_COLS = _DV_END + FOURIER_WIDTH
IN_SPLITS = (_MLA_Q_END, _CKV_END, _KROPE_END, _DQ_END, _DK_END, _DV_END)

kernel_name = "hybrid_mla_diffattn_fnet_prefix_dit_step"


def _rmsnorm(x, g):
    xf = x.astype(jnp.float32)
    y = xf * lax.rsqrt(jnp.mean(xf * xf, axis=-1, keepdims=True) + EPS)
    return (y * g.astype(jnp.float32)).astype(x.dtype)


def _modulation(cond, w_ada_l, b_ada_l):
    mod = jax.nn.silu(cond) @ w_ada_l + b_ada_l
    mod = mod.reshape(cond.shape[:-1] + (1, 6, D_MODEL))
    return tuple(mod[..., i, :] for i in range(6))


def _modulate(x, g, shift, scale):
    return _rmsnorm(x, g) * (1.0 + scale) + shift


def _axial_rope_tables(n, dim):
    rows = n // GRID_W
    row = jnp.broadcast_to(jnp.arange(rows)[:, None], (rows, GRID_W)).reshape(-1).astype(jnp.float32)
    col = jnp.broadcast_to(jnp.arange(GRID_W)[None, :], (rows, GRID_W)).reshape(-1).astype(jnp.float32)
    quarter = dim // 4
    inv = ROPE_BASE ** (-jnp.arange(quarter, dtype=jnp.float32) / quarter)
    ang_r = row[:, None] * inv
    ang_c = col[:, None] * inv
    return (jnp.cos(ang_r), jnp.sin(ang_r), jnp.cos(ang_c), jnp.sin(ang_c))


def _rotate(x, cos, sin):
    x1, x2 = jnp.split(x, 2, axis=-1)
    return jnp.concatenate([x1 * cos - x2 * sin, x2 * cos + x1 * sin], axis=-1)


def _apply_axial_rope(x, tables):
    mid = [1] * (x.ndim - 3)
    cos_r, sin_r, cos_c, sin_c = (t.reshape(t.shape[0], *mid, t.shape[1]).astype(x.dtype) for t in tables)
    xr, xc = jnp.split(x, 2, axis=-1)
    return jnp.concatenate([_rotate(xr, cos_r, sin_r), _rotate(xc, cos_c, sin_c)], axis=-1)


def _map_query_blocks(fn, qs):
    B, N = qs[0].shape[:2]
    nb = N // Q_BLOCK
    blocks = tuple(q.reshape(B, nb, Q_BLOCK, *q.shape[2:]).swapaxes(0, 1) for q in qs)
    out = lax.map(lambda b: fn(*b), blocks)
    return out.swapaxes(0, 1).reshape(B, N, *out.shape[3:])


def _joint_scores(qb, ks, scale):
    s = jnp.concatenate([jnp.einsum('bqhd,bkhd->bhqk', q, k) for q, k in zip(qb, ks)], axis=-1)
    return s.astype(jnp.float32) * scale


def _joint_attention(qs, ks, v, scale):
    def block(*qb):
        p = jax.nn.softmax(_joint_scores(qb, ks, scale), axis=-1)
        return jnp.einsum('bhqk,bkhd->bqhd', p.astype(v.dtype), v)
    return _map_query_blocks(block, tuple(qs))


def _diff_joint_attention(q1s, k1s, q2s, k2s, v, lam, scale):
    n = len(k1s)
    def block(*qb):
        p1 = jax.nn.softmax(_joint_scores(qb[:n], k1s, scale), axis=-1)
        p2 = jax.nn.softmax(_joint_scores(qb[n:], k2s, scale), axis=-1)
        a = p1 - lam * p2
        return jnp.einsum('bhqk,bkhd->bqhd', a.astype(v.dtype), v)
    return _map_query_blocks(block, tuple(q1s) + tuple(q2s))


def _project(h, w_in_l, g_kv_l):
    B, N = h.shape[:2]
    p = h @ w_in_l
    q_m, ckv, krope, q_d, k_d, v_d, u_f = jnp.split(p, IN_SPLITS, axis=-1)
    return (q_m.reshape(B, N, MLA_HEADS, MLA_NOPE + MLA_ROPE),
            _rmsnorm(ckv, g_kv_l),
            krope,
            q_d.reshape(B, N, DIFF_HEADS, 2 * DIFF_DH),
            k_d.reshape(B, N, DIFF_HEADS, 2 * DIFF_DH),
            v_d.reshape(B, N, DIFF_HEADS, DIFF_DV),
            u_f)


def _mla_kv(ckv, krope, w_uk_l, w_uv_l):
    B, L = ckv.shape[:2]
    k_nope = (ckv @ w_uk_l).reshape(B, L, MLA_HEADS, MLA_NOPE)
    v = (ckv @ w_uv_l).reshape(B, L, MLA_HEADS, MLA_DV)
    k_rope = jnp.broadcast_to(krope[:, :, None, :], (B, L, MLA_HEADS, MLA_ROPE))
    return jnp.concatenate([k_nope, k_rope], axis=-1), v


def _fourier_mix(u):
    B, N = u.shape[:2]
    uf = u.reshape(B, N, FOURIER_GROUPS, FOURIER_GROUP_W).astype(jnp.float32)
    y = jnp.fft.fft2(uf, axes=(1, 3), norm="ortho").real
    return y.reshape(B, N, FOURIER_WIDTH).astype(u.dtype)


def _merge(o_m, o_d, u_f, g_sub_l, lam_init, w_out_l):
    B, N = o_m.shape[:2]
    o_d = _rmsnorm(o_d, g_sub_l) * (1.0 - lam_init)
    o = jnp.concatenate([o_m.reshape(B, N, -1), o_d.reshape(B, N, -1), _fourier_mix(u_f)], axis=-1)
    return o @ w_out_l


def _conv_ffn(h, w_up_l, w_conv_l, w_down_l):
    N = h.shape[1]
    u = h @ w_up_l
    pad = CONV_W // 2
    up = jnp.pad(u, ((0, 0), (pad, pad), (0, 0)))
    u = sum(up[:, j:j + N] * w_conv_l[j] for j in range(CONV_W))
    gate, val = jnp.split(u, 2, axis=-1)
    return (jax.nn.silu(gate) * val) @ w_down_l


def _context_mixer(h, lw, lam, lam_init):
    q_m, ckv, krope, q_d, k_d, v_d, u_f = _project(h, lw['w_in'], lw['g_kv'])
    k_m, v_m = _mla_kv(ckv, krope, lw['w_uk'], lw['w_uv'])
    o_m = _joint_attention((q_m,), (k_m,), v_m, MLA_SCALE)
    o_d = _diff_joint_attention((q_d[..., :DIFF_DH],), (k_d[..., :DIFF_DH],),
                                (q_d[..., DIFF_DH:],), (k_d[..., DIFF_DH:],), v_d, lam, DIFF_SCALE)
    o = _merge(o_m, o_d, u_f, lw['g_sub'], lam_init, lw['w_out'])
    return o, ckv, krope, k_d, v_d


def _latent_mixer(h, ckv_c, krope_c, kd_c, vd_c, lw, lam, lam_init, rope_m, rope_d):
    q_m, ckv, krope, q_d, k_d, v_d, u_f = _project(h, lw['w_in'], lw['g_kv'])
    q_nope, q_rope = jnp.split(q_m, [MLA_NOPE], axis=-1)
    q_lat = jnp.concatenate([q_nope, _apply_axial_rope(q_rope, rope_m)], axis=-1)
    k_lat, v_lat = _mla_kv(ckv, _apply_axial_rope(krope, rope_m), lw['w_uk'], lw['w_uv'])
    k_ctx, v_ctx = _mla_kv(ckv_c, krope_c, lw['w_uk'], lw['w_uv'])
    o_m = _joint_attention((q_lat, q_m), (k_lat, k_ctx),
                           jnp.concatenate([v_lat, v_ctx], axis=1), MLA_SCALE)
    B, N = q_d.shape[:2]
    q_rot = _apply_axial_rope(q_d.reshape(B, N, DIFF_HEADS, 2, DIFF_DH), rope_d).reshape(q_d.shape)
    k_rot = _apply_axial_rope(k_d.reshape(B, N, DIFF_HEADS, 2, DIFF_DH), rope_d).reshape(k_d.shape)
    o_d = _diff_joint_attention(
        (q_rot[..., :DIFF_DH], q_d[..., :DIFF_DH]), (k_rot[..., :DIFF_DH], kd_c[..., :DIFF_DH]),
        (q_rot[..., DIFF_DH:], q_d[..., DIFF_DH:]), (k_rot[..., DIFF_DH:], kd_c[..., DIFF_DH:]),
        jnp.concatenate([v_d, vd_c], axis=1), lam, DIFF_SCALE)
    return _merge(o_m, o_d, u_f, lw['g_sub'], lam_init, lw['w_out'])


def setup_inputs(seed: int = 0) -> dict:
    key = jax.random.key(seed)
    ks = jax.random.split(key, 26)
    f32 = jnp.float32

    def nrm(k, shape, scale=1.0):
        return jax.random.normal(k, shape, f32) * scale

    return {
        "x_prompt": nrm(ks[0], (BATCH, SEQ, D_MODEL)),
        "x_sample": nrm(ks[1], (DEC_BATCH, DEC_SEQ, D_MODEL)),
        "c": nrm(ks[2], (DEC_BATCH, D_MODEL)),
        "cache_mla_ckv": nrm(ks[3], (DEC_BATCH, DEPTH, PAST_LEN, MLA_KV_RANK)),
        "cache_mla_krope": nrm(ks[4], (DEC_BATCH, DEPTH, PAST_LEN, MLA_ROPE)),
        "cache_diff_k": nrm(ks[5], (DEC_BATCH, DEPTH, PAST_LEN, DIFF_HEADS, 2 * DIFF_DH)),
        "cache_diff_v": nrm(ks[6], (DEC_BATCH, DEPTH, PAST_LEN, DIFF_HEADS, DIFF_DV)),
        "c_ctx": nrm(ks[7], (D_MODEL,)),
        "w_ada": nrm(ks[8], (DEPTH, D_MODEL, 6 * D_MODEL), 0.5 * D_MODEL ** -0.5),
        "b_ada": nrm(ks[9], (DEPTH, 6 * D_MODEL), 0.01),
        "g_mix_norm": 1.0 + nrm(ks[10], (DEPTH, D_MODEL), 0.02),
        "g_ffn_norm": 1.0 + nrm(ks[11], (DEPTH, D_MODEL), 0.02),
        "w_in": nrm(ks[12], (DEPTH, D_MODEL, IN_COLS), D_MODEL ** -0.5),
        "g_kv_norm": 1.0 + nrm(ks[13], (DEPTH, MLA_KV_RANK), 0.02),
        "w_uk": nrm(ks[14], (DEPTH, MLA_KV_RANK, MLA_HEADS * MLA_NOPE), MLA_KV_RANK ** -0.5),
        "w_uv": nrm(ks[15], (DEPTH, MLA_KV_RANK, MLA_HEADS * MLA_DV), MLA_KV_RANK ** -0.5),
        "lam_q1": nrm(ks[16], (DEPTH, DIFF_DH), 0.1),
        "lam_k1": nrm(ks[17], (DEPTH, DIFF_DH), 0.1),
        "lam_q2": nrm(ks[18], (DEPTH, DIFF_DH), 0.1),
        "lam_k2": nrm(ks[19], (DEPTH, DIFF_DH), 0.1),
        "g_diff_subln": 1.0 + nrm(ks[20], (DEPTH, DIFF_DV), 0.02),
        "w_out": nrm(ks[21], (DEPTH, MIX_WIDTH, D_MODEL), MIX_WIDTH ** -0.5),
        "w_up": nrm(ks[22], (DEPTH, D_MODEL, 2 * D_FF), D_MODEL ** -0.5),
        "w_conv": nrm(ks[23], (DEPTH, CONV_W, 2 * D_FF), CONV_W ** -0.5),
        "w_down": nrm(ks[24], (DEPTH, D_FF, D_MODEL), D_FF ** -0.5),
        "g_final": 1.0 + nrm(ks[25], (D_MODEL,), 0.02),
    }


def reference(x_prompt, x_sample, c, cache_mla_ckv, cache_mla_krope, cache_diff_k, cache_diff_v,
              c_ctx, w_ada, b_ada, g_mix_norm, g_ffn_norm, w_in, g_kv_norm, w_uk, w_uv,
              lam_q1, lam_k1, lam_q2, lam_k2, g_diff_subln, w_out, w_up, w_conv, w_down, g_final):
    n_lat = x_sample.shape[1]
    rope_m = _axial_rope_tables(n_lat, MLA_ROPE)
    rope_d = _axial_rope_tables(n_lat, DIFF_DH)
    x_c = x_prompt
    x_l = x_sample
    ckv_list, krope_list, dk_list, dv_list = [], [], [], []
    for l in range(DEPTH):
        lw = {'w_in': w_in[l], 'g_kv': g_kv_norm[l], 'w_uk': w_uk[l], 'w_uv': w_uv[l],
              'g_sub': g_diff_subln[l], 'w_out': w_out[l]}
        lam_init = 0.8 - 0.6 * math.exp(-0.3 * l)
        lam = (jnp.exp(jnp.sum(lam_q1[l].astype(jnp.float32) * lam_k1[l].astype(jnp.float32)))
               - jnp.exp(jnp.sum(lam_q2[l].astype(jnp.float32) * lam_k2[l].astype(jnp.float32)))
               + lam_init)

        sa, ca, ga, sf, cf, gf = _modulation(c_ctx, w_ada[l], b_ada[l])
        h = _modulate(x_c, g_mix_norm[l], sa, ca)
        o, ckv, krope, k_d, v_d = _context_mixer(h, lw, lam, lam_init)
        x_c = x_c + ga * o
        h = _modulate(x_c, g_ffn_norm[l], sf, cf)
        x_c = x_c + gf * _conv_ffn(h, w_up[l], w_conv[l], w_down[l])
        ckv_list.append(ckv)
        krope_list.append(krope)
        dk_list.append(k_d)
        dv_list.append(v_d)

        sa, ca, ga, sf, cf, gf = _modulation(c, w_ada[l], b_ada[l])
        h = _modulate(x_l, g_mix_norm[l], sa, ca)
        o = _latent_mixer(h, cache_mla_ckv[:, l], cache_mla_krope[:, l], cache_diff_k[:, l],
                          cache_diff_v[:, l], lw, lam, lam_init, rope_m, rope_d)
        x_l = x_l + ga * o
        h = _modulate(x_l, g_ffn_norm[l], sf, cf)
        x_l = x_l + gf * _conv_ffn(h, w_up[l], w_conv[l], w_down[l])

    y_prompt = _rmsnorm(x_c, g_final)
    y_sample = _rmsnorm(x_l, g_final)
    new_mla_ckv = jnp.stack(ckv_list, axis=1)
    new_mla_krope = jnp.stack(krope_list, axis=1)
    new_diff_k = jnp.stack(dk_list, axis=1)
    new_diff_v = jnp.stack(dv_list, axis=1)
    return (y_prompt, y_sample, new_mla_ckv, new_mla_krope, new_diff_k, new_diff_v)
```

```python
import functools
import math

import numpy as np
import jax
import jax.numpy as jnp
from jax import lax
from jax.experimental import pallas as pl
from jax.experimental.pallas import tpu as pltpu

F32 = jnp.float32
BF16 = jnp.bfloat16

D_MODEL = 1024
GRID_W = 64
HEADS = 6
MLA_NOPE, MLA_ROPE, MLA_DV, MLA_RANK = 64, 32, 64, 128
DIFF_DH, DIFF_DV = 32, 64
F_GROUPS, F_GROUP_W = 4, 64
F_WIDTH = F_GROUPS * F_GROUP_W
D_FF = 2816
ROPE_BASE = 10000.0
EPS = 1e-6
MLA_SCALE = (MLA_NOPE + MLA_ROPE) ** -0.5
DIFF_SCALE = DIFF_DH ** -0.5

_QM0 = 0
_CKV0 = HEADS * (MLA_NOPE + MLA_ROPE)
_KR0 = _CKV0 + MLA_RANK
_QD0 = _KR0 + MLA_ROPE
_KD0 = _QD0 + HEADS * 2 * DIFF_DH
_VD0 = _KD0 + HEADS * 2 * DIFF_DH
_UF0 = _VD0 + HEADS * DIFF_DV

LANE = 128
HW = HEADS * LANE
TM = 256
TQ = 256
FF_CHUNK = 256
N_FF_CHUNKS = D_FF // FF_CHUNK
VMEM_LIMIT = 56 * 1024 * 1024


def _cparams(sem):
    return pltpu.CompilerParams(dimension_semantics=sem, vmem_limit_bytes=VMEM_LIMIT)


def _w1_columns(ctx):
    z = lambda n: [-1] * n
    r = lambda a, n: list(range(a, a + n))
    cols = []
    cols += r(_CKV0, MLA_RANK)
    cols += z(MLA_NOPE) + r(_KR0, MLA_ROPE) + z(LANE - MLA_NOPE - MLA_ROPE)
    for h in range(HEADS):
        b = _QM0 + h * (MLA_NOPE + MLA_ROPE)
        cols += r(b, MLA_NOPE) + r(b + MLA_NOPE, MLA_ROPE) + r(b + MLA_NOPE, MLA_ROPE)
    for h in range(HEADS):
        b = _QD0 + h * 2 * DIFF_DH
        cols += r(b, DIFF_DH) * 2 + r(b + DIFF_DH, DIFF_DH) * 2
    for h in range(HEADS):
        b = _KD0 + h * 2 * DIFF_DH
        cols += r(b, DIFF_DH) + z(DIFF_DH) + r(b + DIFF_DH, DIFF_DH) + z(DIFF_DH)
    for h in range(HEADS):
        cols += z(LANE - DIFF_DV) + r(_VD0 + h * DIFF_DV, DIFF_DV)
    cols += r(_UF0, F_WIDTH)
    if ctx:
        cols += r(_KD0, HEADS * 2 * DIFF_DH) + r(_VD0, HEADS * DIFF_DV)
        cols += r(_KR0, MLA_ROPE) + z(LANE - MLA_ROPE)
    return np.asarray(cols, np.int32)


_W1_COLS = {False: _w1_columns(False), True: _w1_columns(True)}
_C_CKV, _C_KR, _C_QM = 0, LANE, 2 * LANE
_C_QD = _C_QM + HW
_C_KD = _C_QD + HW
_C_VD = _C_KD + HW
_C_UF = _C_VD + HW
_C_XKD = _C_UF + F_WIDTH
_C_XVD = _C_XKD + HEADS * 2 * DIFF_DH
_C_XKR = _C_XVD + HEADS * DIFF_DV


def _w2_columns():
    cols = []
    for base in (0, HEADS * MLA_NOPE):
        for h in range(HEADS):
            cols += list(range(base + h * 64, base + (h + 1) * 64)) + [-1] * 64
    return np.asarray(cols, np.int32)


_W2_COLS = _w2_columns()


def _wout_rows():
    rows = []
    for h in range(HEADS):
        rows += list(range(h * MLA_DV, (h + 1) * MLA_DV))
        rows += list(range(HEADS * MLA_DV + h * DIFF_DV, HEADS * MLA_DV + (h + 1) * DIFF_DV))
    return np.asarray(rows, np.int32)


_WOUT_ROWS = _wout_rows()


def _take_cols(w, cols):
    g = jnp.take(w, jnp.asarray(np.maximum(cols, 0)), axis=-1)
    return g * jnp.asarray((cols >= 0).astype(np.float32))


def _selection(src_of_col, n_src):
    e = np.zeros((n_src, len(src_of_col)), np.float32)
    for c, s in enumerate(src_of_col):
        if s >= 0:
            e[s, c] = 1.0
    return e


def _cache_selections():
    kr, kd, vd = [], [], []
    for h in range(HEADS):
        kr += [-1] * (MLA_NOPE + MLA_ROPE) + list(range(MLA_ROPE))
        b = h * 2 * DIFF_DH
        kd += [-1] * DIFF_DH + list(range(b, b + DIFF_DH)) + [-1] * DIFF_DH + list(range(b + DIFF_DH, b + 2 * DIFF_DH))
        vd += [-1] * (LANE - DIFF_DV) + list(range(h * DIFF_DV, (h + 1) * DIFF_DV))
    return (_selection(kr, MLA_ROPE), _selection(kd, HEADS * 2 * DIFF_DH), _selection(vd, HEADS * DIFF_DV))


_E_KR, _E_KD, _E_VD = _cache_selections()


def _rope_tables(n_lat):
    t = np.arange(n_lat)
    row = (t // GRID_W).astype(np.float64)
    col = (t % GRID_W).astype(np.float64)
    quarter = MLA_ROPE // 4
    inv = ROPE_BASE ** (-np.arange(quarter, dtype=np.float64) / quarter)
    inv = inv.astype(np.float32).astype(np.float64)
    ang = np.concatenate([row[:, None] * inv, row[:, None] * inv, col[:, None] * inv, col[:, None] * inv], axis=1)
    ang = ang.astype(np.float32)
    cos32 = np.cos(ang).astype(np.float32)
    sign = np.concatenate([-np.ones(quarter), np.ones(quarter)] * 2).astype(np.float32)
    sin32 = np.sin(ang).astype(np.float32) * sign
    one, zero = np.ones((n_lat, 32), np.float32), np.zeros((n_lat, 32), np.float32)
    cm = np.concatenate([one, one, cos32, one], axis=1)
    sm = np.concatenate([zero, zero, sin32, zero], axis=1)
    cd = np.concatenate([cos32, one, cos32, one], axis=1)
    sd = np.concatenate([sin32, zero, sin32, zero], axis=1)
    return cm, sm, cd, sd


def _mod_kernel(cond_ref, w_ref, b_ref, q1_ref, k1_ref, q2_ref, k2_ref, li_ref, mod_ref, lam_ref):
    a = cond_ref[...]
    a = a * jax.nn.sigmoid(a)
    acc = jnp.dot(a.astype(BF16), w_ref[...].astype(BF16), preferred_element_type=F32)
    mod_ref[...] = acc + b_ref[...]
    d1 = jnp.sum(q1_ref[...] * k1_ref[...], axis=-1, keepdims=True)
    d2 = jnp.sum(q2_ref[...] * k2_ref[...], axis=-1, keepdims=True)
    lam_ref[...] = jnp.exp(d1) - jnp.exp(d2) + li_ref[...]


def _modulation_call(cond, w_ada, b_ada, lam_q1, lam_k1, lam_q2, lam_k2, lam_init_rows):
    depth = w_ada.shape[0]
    n_sec = w_ada.shape[2] // D_MODEL
    rows = cond.shape[0]
    vec = lambda a: a.reshape(depth, 1, a.shape[-1])
    lam_spec = pl.BlockSpec((None, 1, DIFF_DH), lambda l, s: (l, 0, 0))
    return pl.pallas_call(
        _mod_kernel,
        grid=(depth, n_sec),
        in_specs=[
            pl.BlockSpec((rows, D_MODEL), lambda l, s: (0, 0)),
            pl.BlockSpec((None, D_MODEL, D_MODEL), lambda l, s: (l, 0, s)),
            pl.BlockSpec((None, 1, D_MODEL), lambda l, s: (l, 0, s)),
            lam_spec, lam_spec, lam_spec, lam_spec,
            pl.BlockSpec((None, 1, LANE), lambda l, s: (l, 0, 0)),
        ],
        out_specs=[
            pl.BlockSpec((None, rows, D_MODEL), lambda l, s: (l, 0, s)),
            pl.BlockSpec((None, 1, LANE), lambda l, s: (l, 0, 0)),
        ],
        out_shape=[
            jax.ShapeDtypeStruct((depth, rows, n_sec * D_MODEL), F32),
            jax.ShapeDtypeStruct((depth, 1, LANE), F32),
        ],
        compiler_params=_cparams(("arbitrary", "arbitrary")),
        name="modulation",
    )(cond, w_ada, vec(b_ada), vec(lam_q1), vec(lam_k1), vec(lam_q2), vec(lam_k2), lam_init_rows)


def _rms(x):
    return x * lax.rsqrt(jnp.mean(x * x, axis=-1, keepdims=True) + EPS)


def _swap_halves(x):
    lane = lax.broadcasted_iota(jnp.int32, x.shape, 1)
    first = (lane & 8) == 0
    return jnp.where(first, pltpu.roll(x, LANE - 8, axis=1), pltpu.roll(x, 8, axis=1))


def _rope(x, c, s):
    return x * c + _swap_halves(x) * s


def _inproj_kernel(*refs, rope, ctx, tiles_per_seq):
    it = iter(refs)
    l_ref = next(it)
    x_ref, shift_ref, scale_ref, g_ref, w1_ref, gkv_ref, w2_ref, cs_ref = (next(it) for _ in range(8))
    if rope:
        cm_ref, sm_ref, cd_ref, sd_ref = (next(it) for _ in range(4))
    qm_ref, km_ref, qd_ref, kd_ref, v_ref, ab_ref = (next(it) for _ in range(6))
    if ctx:
        ckv_o, kr_o, kdc_o, vdc_o = (next(it) for _ in range(4))
    h_scr = next(it)
    del l_ref

    x = x_ref[...]
    h = _rms(x) * g_ref[...] * (1.0 + scale_ref[...]) + shift_ref[...]
    h_scr[...] = h.astype(BF16)

    if rope:
        r0 = pl.multiple_of((pl.program_id(0) % tiles_per_seq) * TM, TM)
        cm, sm = cm_ref[pl.ds(r0, TM), :], sm_ref[pl.ds(r0, TM), :]
        cd, sd = cd_ref[pl.ds(r0, TM), :], sd_ref[pl.ds(r0, TM), :]

    def proj(c0, width):
        return jnp.dot(h_scr[...], w1_ref[:, c0:c0 + width], preferred_element_type=F32)

    acc = proj(_C_CKV, 2 * LANE)
    ckvn = _rms(acc[:, :LANE]) * gkv_ref[...]
    kr = acc[:, LANE:]
    if ctx:
        ckv_o[...] = ckvn
    if rope:
        kr = _rope(kr, cm, sm)
    kv2 = jnp.dot(ckvn.astype(BF16), w2_ref[...], preferred_element_type=F32)
    for hd in range(HEADS):
        km_ref[:, hd * LANE:(hd + 1) * LANE] = (kv2[:, hd * LANE:(hd + 1) * LANE] + kr).astype(BF16)

    def blocked(c_base, out_ref, c_tab, s_tab, scale, add=None):
        for j in range(HW // (2 * LANE)):
            acc = proj(c_base + j * 2 * LANE, 2 * LANE)
            for k in range(2):
                blk = acc[:, k * LANE:(k + 1) * LANE]
                col = (2 * j + k) * LANE
                if rope and c_tab is not None:
                    blk = _rope(blk, c_tab, s_tab)
                if scale is not None:
                    blk = blk * scale
                if add is not None:
                    blk = blk + add[:, col:col + LANE]
                out_ref[:, col:col + LANE] = blk.astype(BF16)

    blocked(_C_QM, qm_ref, cm if rope else None, sm if rope else None, MLA_SCALE)
    blocked(_C_QD, qd_ref, cd if rope else None, sd if rope else None, DIFF_SCALE)
    blocked(_C_KD, kd_ref, cd if rope else None, sd if rope else None, None)
    blocked(_C_VD, v_ref, None, None, None, add=kv2[:, HW:])

    uf = proj(_C_UF, F_WIDTH)
    ab_ref[...] = jnp.dot(uf.astype(BF16), cs_ref[...], preferred_element_type=F32).astype(BF16)

    if ctx:
        kdc_o[...] = proj(_C_XKD, HEADS * 2 * DIFF_DH)
        vdc_o[...] = proj(_C_XVD, HEADS * DIFF_DV)
        kr_o[...] = proj(_C_XKR, LANE)[:, :MLA_ROPE]


def _inproj_call(l_arr, x, mod5, b_row, g_mix, w1, g_kv, w2, cs64, tables, seq_len, ctx):
    t = x.shape[0]
    n_tiles = t // TM
    tiles_per_seq = seq_len // TM
    rope = tables is not None
    n1 = w1.shape[-1]
    row = lambda w: pl.BlockSpec((TM, w), lambda i, l: (i, 0))
    modspec = lambda sec: pl.BlockSpec((None, None, None, 1, D_MODEL), lambda i, l: (l[0], b_row(i), sec, 0, 0))
    in_specs = [
        row(D_MODEL), modspec(0), modspec(1),
        pl.BlockSpec((None, 1, D_MODEL), lambda i, l: (l[0], 0, 0)),
        pl.BlockSpec((None, D_MODEL, n1), lambda i, l: (l[0], 0, 0)),
        pl.BlockSpec((None, 1, MLA_RANK), lambda i, l: (l[0], 0, 0)),
        pl.BlockSpec((None, MLA_RANK, 2 * HW), lambda i, l: (l[0], 0, 0)),
        pl.BlockSpec((F_WIDTH, 2 * F_WIDTH), lambda i, l: (0, 0)),
    ]
    args = [x, mod5, mod5, g_mix, w1, g_kv, w2, cs64]
    if rope:
        in_specs += [pl.BlockSpec((seq_len, LANE), lambda i, l: (0, 0))] * 4
        args += list(tables)
    out_specs = [row(HW)] * 5 + [row(2 * F_WIDTH)]
    out_shape = [jax.ShapeDtypeStruct((t, HW), BF16)] * 5 + [jax.ShapeDtypeStruct((t, 2 * F_WIDTH), BF16)]
    if ctx:
        widths = (MLA_RANK, MLA_ROPE, HEADS * 2 * DIFF_DH, HEADS * DIFF_DV)
        out_specs += [row(w) for w in widths]
        out_shape += [jax.ShapeDtypeStruct((t, w), F32) for w in widths]
    return pl.pallas_call(
        functools.partial(_inproj_kernel, rope=rope, ctx=ctx, tiles_per_seq=tiles_per_seq),
        grid_spec=pltpu.PrefetchScalarGridSpec(
            num_scalar_prefetch=1, grid=(n_tiles,), in_specs=in_specs, out_specs=out_specs,
            scratch_shapes=[pltpu.VMEM((TM, D_MODEL), BF16)]),
        out_shape=out_shape,
        compiler_params=_cparams(("arbitrary",)),
        name="inproj_ctx" if ctx else "inproj_lat",
    )(l_arr, *args)


def _cacheprep_kernel(ckv_ref, kr_ref, kd_ref, vd_ref, w2_ref, ekr_ref, ekd_ref, evd_ref, km_ref, kdo_ref, v_ref):
    dot = lambda a, b: jnp.dot(a.astype(BF16), b, preferred_element_type=F32)
    kv2 = dot(ckv_ref[...], w2_ref[...])
    km_ref[...] = (kv2[:, :HW] + dot(kr_ref[...], ekr_ref[...])).astype(BF16)
    kdo_ref[...] = dot(kd_ref[...], ekd_ref[...]).astype(BF16)
    v_ref[...] = (kv2[:, HW:] + dot(vd_ref[...], evd_ref[...])).astype(BF16)


def _cacheprep_call(cache_ckv, cache_kr, cache_kd, cache_vd, w2, ekr, ekd, evd):
    b, depth, p = cache_ckv.shape[:3]
    cspec = lambda w: pl.BlockSpec((None, None, p, w), lambda l, i: (i, l, 0, 0))
    const = lambda a: pl.BlockSpec(a.shape, lambda l, i: (0,) * a.ndim)
    ospec = pl.BlockSpec((None, None, p, HW), lambda l, i: (l, i, 0, 0))
    oshape = jax.ShapeDtypeStruct((depth, b, p, HW), BF16)
    return pl.pallas_call(
        _cacheprep_kernel,
        grid=(depth, b),
        in_specs=[cspec(MLA_RANK), cspec(MLA_ROPE), cspec(HEADS * 2 * DIFF_DH), cspec(HEADS * DIFF_DV),
                  pl.BlockSpec((None, MLA_RANK, 2 * HW), lambda l, i: (l, 0, 0)),
                  const(ekr), const(ekd), const(evd)],
        out_specs=[ospec] * 3,
        out_shape=[oshape] * 3,
        compiler_params=_cparams(("arbitrary", "arbitrary")),
        name="cacheprep",
    )(cache_ckv, cache_kr, cache_kd, cache_vd, w2, ekr, ekd, evd)


def _softmax_pv(q, k_refs, v_refs):
    nt = (((1,), (1,)), ((), ()))
    s = [lax.dot_general(q, k[...], nt, preferred_element_type=F32) for k in k_refs]
    m = functools.reduce(jnp.maximum, [jnp.max(si, axis=-1, keepdims=True) for si in s])
    p = [jnp.exp(si - m) for si in s]
    den = functools.reduce(jnp.add, [jnp.sum(pi, axis=-1, keepdims=True) for pi in p])
    o = functools.reduce(jnp.add, [jnp.dot(pi.astype(BF16), v[...], preferred_element_type=F32)
                                   for pi, v in zip(p, v_refs)])
    return o / den


def _attn_kernel(*refs, n_seg):
    it = iter(refs)
    l_ref = next(it)
    lam_ref, gsub_ref, sub_ref, qm_ref, qd_ref = (next(it) for _ in range(5))
    km = [next(it) for _ in range(n_seg)]
    kd = [next(it) for _ in range(n_seg)]
    vv = [next(it) for _ in range(n_seg)]
    o_ref = next(it)
    del l_ref

    lane = lax.broadcasted_iota(jnp.int32, (TQ, LANE), 1)
    upper = lane >= LANE // 2
    o_m = _softmax_pv(qm_ref[...], km, vv)
    qd = qd_ref[...]
    zero = jnp.zeros_like(qd)
    o_1 = _softmax_pv(jnp.where(upper, zero, qd), kd, vv)
    o_2 = _softmax_pv(jnp.where(upper, qd, zero), kd, vv)
    o_d = jnp.where(upper, o_1 - lam_ref[...] * o_2, 0.0)
    ms = jnp.sum(o_d * o_d, axis=-1, keepdims=True) * (1.0 / DIFF_DV)
    o_d = o_d * lax.rsqrt(ms + EPS) * gsub_ref[...] * sub_ref[...]
    o_ref[...] = jnp.where(upper, o_d, o_m).astype(BF16)


def _attn_call(l_arr, lam, gsub, sub, qm, qd, segs, batch, n_q):
    n_seg = len(segs)
    q3 = lambda a: a.reshape(batch, n_q, HW)
    layer_vec = pl.BlockSpec((None, 1, LANE), lambda b, h, i, l: (l[0], 0, 0))
    qspec = pl.BlockSpec((None, TQ, LANE), lambda b, h, i, l: (b, i, h))
    kspecs, kargs = [], []
    for which in range(3):
        for seg in segs:
            a = seg[which]
            if a.ndim == 4:
                kspecs.append(pl.BlockSpec((None, None, a.shape[2], LANE), lambda b, h, i, l: (l[0], b, 0, h)))
            else:
                kspecs.append(pl.BlockSpec((None, a.shape[1], LANE), lambda b, h, i, l: (b, 0, h)))
            kargs.append(a)
    return pl.pallas_call(
        functools.partial(_attn_kernel, n_seg=n_seg),
        grid_spec=pltpu.PrefetchScalarGridSpec(
            num_scalar_prefetch=1, grid=(batch, HEADS, n_q // TQ),
            in_specs=[layer_vec, layer_vec, layer_vec, qspec, qspec] + kspecs,
            out_specs=qspec),
        out_shape=jax.ShapeDtypeStruct((batch, n_q, HW), BF16),
        compiler_params=_cparams(("arbitrary", "arbitrary", "arbitrary")),
        name="attention_%dseg" % n_seg,
    )(l_arr, lam, gsub, sub, q3(qm), q3(qd), *kargs).reshape(batch * n_q, HW)


def _dft_kernel(c_ref, s_ref, ab_ref, y_ref):
    ya = jnp.dot(c_ref[...], ab_ref[:, :F_WIDTH], preferred_element_type=F32)
    yb = jnp.dot(s_ref[...], ab_ref[:, F_WIDTH:], preferred_element_type=F32)
    y_ref[...] = (ya - yb).astype(BF16)


def _dft_call(cn, sn, ab, batch, n):
    tr = min(n, TM)
    mat = pl.BlockSpec((tr, n), lambda b, r: (r, 0))
    return pl.pallas_call(
        _dft_kernel,
        grid=(batch, n // tr),
        in_specs=[mat, mat, pl.BlockSpec((None, n, 2 * F_WIDTH), lambda b, r: (b, 0, 0))],
        out_specs=pl.BlockSpec((None, tr, F_WIDTH), lambda b, r: (b, r, 0)),
        out_shape=jax.ShapeDtypeStruct((batch, n, F_WIDTH), BF16),
        compiler_params=_cparams(("arbitrary", "arbitrary")),
        name="position_dft",
    )(cn, sn, ab.reshape(batch, n, 2 * F_WIDTH)).reshape(batch * n, F_WIDTH)


def _outproj_kernel(l_ref, o_ref, y_ref, x_ref, gate_ref, shift_ref, scale_ref, g_ref, wa_ref, wf_ref,
                    xo_ref, h_ref):
    del l_ref
    o = jnp.dot(o_ref[...], wa_ref[...], preferred_element_type=F32)
    o = o + jnp.dot(y_ref[...], wf_ref[...], preferred_element_type=F32)
    x = x_ref[...] + gate_ref[...] * o
    xo_ref[...] = x
    h_ref[...] = (_rms(x) * g_ref[...] * (1.0 + scale_ref[...]) + shift_ref[...]).astype(BF16)


def _outproj_call(l_arr, o, y, x, mod5, b_row, g_ffn, wo_a, wo_f):
    t = x.shape[0]
    row = lambda w: pl.BlockSpec((TM, w), lambda i, l: (i, 0))
    modspec = lambda sec: pl.BlockSpec((None, None, None, 1, D_MODEL), lambda i, l: (l[0], b_row(i), sec, 0, 0))
    return pl.pallas_call(
        _outproj_kernel,
        grid_spec=pltpu.PrefetchScalarGridSpec(
            num_scalar_prefetch=1, grid=(t // TM,),
            in_specs=[row(HW), row(F_WIDTH), row(D_MODEL), modspec(2), modspec(3), modspec(4),
                      pl.BlockSpec((None, 1, D_MODEL), lambda i, l: (l[0], 0, 0)),
                      pl.BlockSpec((None, HW, D_MODEL), lambda i, l: (l[0], 0, 0)),
                      pl.BlockSpec((None, F_WIDTH, D_MODEL), lambda i, l: (l[0], 0, 0))],
            out_specs=[row(D_MODEL), row(D_MODEL)]),
        out_shape=[jax.ShapeDtypeStruct((t, D_MODEL), F32), jax.ShapeDtypeStruct((t, D_MODEL), BF16)],
        compiler_params=_cparams(("arbitrary",)),
        name="outproj",
    )(l_arr, o, y, x, mod5, mod5, mod5, g_ffn, wo_a, wo_f)


def _ffn_kernel(l_ref, h_ref, hp_ref, hn_ref, x_ref, gate_ref, gfin_ref, wup_ref, wconv_ref, wdn_ref,
                xo_ref, acc_ref, *, tiles_per_seq, final):
    del l_ref
    i = pl.program_id(0)
    has_prev = ((i % tiles_per_seq) != 0).astype(F32)
    has_next = ((i % tiles_per_seq) != tiles_per_seq - 1).astype(F32)
    row = lax.broadcasted_iota(jnp.int32, (TM, 2 * FF_CHUNK), 0)
    is_first, is_last = row == 0, row == TM - 1
    hmain = h_ref[...]
    hprev = hp_ref[...]
    hnext = hn_ref[...]
    acc_ref[...] = jnp.zeros_like(acc_ref)

    def chunk(c, carry):
        w = wup_ref[c]
        u = jnp.dot(hmain, w, preferred_element_type=F32)
        u_prev = jnp.dot(hprev, w, preferred_element_type=F32)[7:8, :] * has_prev
        u_next = jnp.dot(hnext, w, preferred_element_type=F32)[0:1, :] * has_next
        wc = wconv_ref[c]
        down = jnp.where(is_first, u_prev, pltpu.roll(u, 1, axis=0))
        up = jnp.where(is_last, u_next, pltpu.roll(u, TM - 1, axis=0))
        u = down * wc[0:1, :] + u * wc[1:2, :] + up * wc[2:3, :]
        gate, val = u[:, :FF_CHUNK], u[:, FF_CHUNK:]
        act = (gate * jax.nn.sigmoid(gate) * val).astype(BF16)
        acc_ref[...] += jnp.dot(act, wdn_ref[c], preferred_element_type=F32)
        return carry

    lax.fori_loop(0, N_FF_CHUNKS, chunk, 0)
    x = x_ref[...] + gate_ref[...] * acc_ref[...]
    if final:
        x = _rms(x) * gfin_ref[...]
    xo_ref[...] = x


def _ffn_call(l_arr, h, x, mod5, b_row, g_final, wup, wconv, wdn, seq_len, final):
    t = x.shape[0]
    n_tiles = t // TM
    halo = TM // 8
    row = lambda w: pl.BlockSpec((TM, w), lambda i, l: (i, 0))
    whole = lambda a: pl.BlockSpec((None,) + a.shape[1:], lambda i, l: (l[0],) + (0,) * (a.ndim - 1))
    return pl.pallas_call(
        functools.partial(_ffn_kernel, tiles_per_seq=seq_len // TM, final=final),
        grid_spec=pltpu.PrefetchScalarGridSpec(
            num_scalar_prefetch=1, grid=(n_tiles,),
            in_specs=[row(D_MODEL),
                      pl.BlockSpec((8, D_MODEL), lambda i, l: (jnp.maximum(i * halo - 1, 0), 0)),
                      pl.BlockSpec((8, D_MODEL), lambda i, l: (jnp.minimum((i + 1) * halo, n_tiles * halo - 1), 0)),
                      row(D_MODEL),
                      pl.BlockSpec((None, None, None, 1, D_MODEL), lambda i, l: (l[0], b_row(i), 5, 0, 0)),
                      pl.BlockSpec((1, D_MODEL), lambda i, l: (0, 0)),
                      whole(wup), whole(wconv), whole(wdn)],
            out_specs=row(D_MODEL),
            scratch_shapes=[pltpu.VMEM((TM, D_MODEL), F32)]),
        out_shape=jax.ShapeDtypeStruct((t, D_MODEL), F32),
        compiler_params=_cparams(("arbitrary",)),
        name="ffn_final" if final else "ffn",
    )(l_arr, h, h, h, x, mod5, g_final, wup, wconv, wdn)


def _dft_mats(n):
    k = jnp.arange(n, dtype=jnp.int32)
    ang = ((k[:, None] * k[None, :]) % n).astype(F32) * (2.0 * math.pi / n)
    s = 1.0 / math.sqrt(n)
    return (jnp.cos(ang) * s).astype(BF16), (jnp.sin(ang) * s).astype(BF16)


def _channel_dft():
    k = np.arange(F_GROUP_W)
    ang = 2.0 * np.pi * ((k[:, None] * k[None, :]) % F_GROUP_W) / F_GROUP_W
    eye = np.eye(F_GROUPS)
    s = 1.0 / math.sqrt(F_GROUP_W)
    return np.concatenate([np.kron(eye, np.cos(ang) * s), np.kron(eye, np.sin(ang) * s)], axis=1).astype(np.float32)


def kernel(x_prompt, x_sample, c, cache_mla_ckv, cache_mla_krope, cache_diff_k, cache_diff_v, c_ctx, w_ada, b_ada,
           g_mix_norm, g_ffn_norm, w_in, g_kv_norm, w_uk, w_uv, lam_q1, lam_k1, lam_q2, lam_k2, g_diff_subln,
           w_out, w_up, w_conv, w_down, g_final):
    batch_c, seq_c, _ = x_prompt.shape
    batch_l, seq_l, _ = x_sample.shape
    depth = w_in.shape[0]
    past = cache_mla_ckv.shape[2]
    assert seq_c % TM == 0 and seq_l % TM == 0 and seq_l % GRID_W == 0 and D_FF % FF_CHUNK == 0

    w1_c = _take_cols(w_in, _W1_COLS[True]).astype(BF16)
    w1_l = w1_c[:, :, :_C_XKD]
    w2 = _take_cols(jnp.concatenate([w_uk, w_uv], axis=-1), _W2_COLS).astype(BF16)
    wo = jnp.take(w_out, jnp.asarray(_WOUT_ROWS), axis=1)
    wo_a = wo.astype(BF16)
    wo_f = w_out[:, HEADS * (MLA_DV + DIFF_DV):, :].astype(BF16)
    wup = jnp.concatenate([w_up[:, :, :D_FF].reshape(depth, D_MODEL, N_FF_CHUNKS, FF_CHUNK),
                           w_up[:, :, D_FF:].reshape(depth, D_MODEL, N_FF_CHUNKS, FF_CHUNK)], axis=-1)
    wup = wup.transpose(0, 2, 1, 3).astype(BF16)
    wconv = jnp.concatenate([w_conv[:, :, :D_FF].reshape(depth, 3, N_FF_CHUNKS, FF_CHUNK),
                             w_conv[:, :, D_FF:].reshape(depth, 3, N_FF_CHUNKS, FF_CHUNK)], axis=-1)
    wconv = wconv.transpose(0, 2, 1, 3)
    wdn = w_down.reshape(depth, N_FF_CHUNKS, FF_CHUNK, D_MODEL).astype(BF16)
    vec = lambda a: a.reshape(depth, 1, a.shape[-1])
    g_mix, g_ffn, g_kv = vec(g_mix_norm), vec(g_ffn_norm), vec(g_kv_norm)
    gsub = jnp.concatenate([jnp.ones_like(g_diff_subln), g_diff_subln], axis=-1).reshape(depth, 1, LANE)
    lam_init = np.asarray([0.8 - 0.6 * math.exp(-0.3 * l) for l in range(depth)], np.float32)
    lam_init_rows = jnp.asarray(np.broadcast_to(lam_init[:, None, None], (depth, 1, LANE)))
    sub_rows = jnp.asarray(np.broadcast_to((1.0 - lam_init)[:, None, None], (depth, 1, LANE)).astype(np.float32))
    g_fin = g_final.reshape(1, D_MODEL)
    cs64 = jnp.asarray(_channel_dft()).astype(BF16)
    tables = tuple(jnp.asarray(a) for a in _rope_tables(seq_l))
    dft_c, dft_l = _dft_mats(seq_c), _dft_mats(seq_l)

    n_rows = -(-(batch_l + 1) // 8) * 8
    cond = jnp.zeros((n_rows, D_MODEL), F32).at[:batch_l].set(c).at[batch_l].set(c_ctx)
    mod, lam = _modulation_call(cond, w_ada, b_ada, lam_q1, lam_k1, lam_q2, lam_k2, lam_init_rows)
    mod5 = mod.reshape(depth, n_rows, w_ada.shape[2] // D_MODEL, 1, D_MODEL)
    tiles_l = seq_l // TM
    row_l = lambda i: i // tiles_l
    row_c = lambda i: batch_l

    flat = lambda a: a.reshape(a.shape[0], a.shape[1], a.shape[2], -1)
    km_p, kd_p, v_p = _cacheprep_call(cache_mla_ckv, cache_mla_krope, flat(cache_diff_k), flat(cache_diff_v), w2,
                                      jnp.asarray(_E_KR).astype(BF16), jnp.asarray(_E_KD).astype(BF16),
                                      jnp.asarray(_E_VD).astype(BF16))

    x_c = x_prompt.reshape(batch_c * seq_c, D_MODEL)
    x_l = x_sample.reshape(batch_l * seq_l, D_MODEL)
    new_ckv, new_kr, new_kd, new_vd = [], [], [], []
    for l in range(depth):
        l_arr = jnp.full((1,), l, jnp.int32)
        final = l == depth - 1
        qm, km, qd, kd, v, ab, ckv_o, kr_o, kd_o, vd_o = _inproj_call(
            l_arr, x_c, mod5, row_c, g_mix, w1_c, g_kv, w2, cs64, None, seq_c, True)
        k3 = lambda a: a.reshape(batch_c, seq_c, HW)
        o = _attn_call(l_arr, lam, gsub, sub_rows, qm, qd, [(k3(km), k3(kd), k3(v))], batch_c, seq_c)
        y = _dft_call(*dft_c, ab, batch_c, seq_c)
        x_c, h2 = _outproj_call(l_arr, o, y, x_c, mod5, row_c, g_ffn, wo_a, wo_f)
        x_c = _ffn_call(l_arr, h2, x_c, mod5, row_c, g_fin, wup, wconv, wdn, seq_c, final)
        new_ckv.append(ckv_o)
        new_kr.append(kr_o)
        new_kd.append(kd_o)
        new_vd.append(vd_o)
        qm, km, qd, kd, v, ab = _inproj_call(
            l_arr, x_l, mod5, row_l, g_mix, w1_l, g_kv, w2, cs64, tables, seq_l, False)
        k3 = lambda a: a.reshape(batch_l, seq_l, HW)
        o = _attn_call(l_arr, lam, gsub, sub_rows, qm, qd, [(k3(km), k3(kd), k3(v)), (km_p, kd_p, v_p)],
                       batch_l, seq_l)
        y = _dft_call(*dft_l, ab, batch_l, seq_l)
        x_l, h2 = _outproj_call(l_arr, o, y, x_l, mod5, row_l, g_ffn, wo_a, wo_f)
        x_l = _ffn_call(l_arr, h2, x_l, mod5, row_l, g_fin, wup, wconv, wdn, seq_l, final)

    stack = lambda parts, shape: jnp.stack([p.reshape((batch_c, seq_c) + shape) for p in parts], axis=1)
    return (x_c.reshape(batch_c, seq_c, D_MODEL),
            x_l.reshape(batch_l, seq_l, D_MODEL),
            stack(new_ckv, (MLA_RANK,)),
            stack(new_kr, (MLA_ROPE,)),
            stack(new_kd, (HEADS, 2 * DIFF_DH)),
            stack(new_vd, (HEADS, DIFF_DV)))
```

```python
import functools
import math

import numpy as np
import jax
import jax.numpy as jnp
from jax import lax
from jax.experimental import pallas as pl
from jax.experimental.pallas import tpu as pltpu

F32 = jnp.float32
BF16 = jnp.bfloat16

D_MODEL = 1024
GRID_W = 64
HEADS = 6
MLA_NOPE, MLA_ROPE, MLA_DV, MLA_RANK = 64, 32, 64, 128
DIFF_DH, DIFF_DV = 32, 64
F_GROUPS, F_GROUP_W = 4, 64
F_WIDTH = F_GROUPS * F_GROUP_W
D_FF = 2816
ROPE_BASE = 10000.0
EPS = 1e-6
MLA_SCALE = (MLA_NOPE + MLA_ROPE) ** -0.5
DIFF_SCALE = DIFF_DH ** -0.5

_QM0 = 0
_CKV0 = HEADS * (MLA_NOPE + MLA_ROPE)
_KR0 = _CKV0 + MLA_RANK
_QD0 = _KR0 + MLA_ROPE
_KD0 = _QD0 + HEADS * 2 * DIFF_DH
_VD0 = _KD0 + HEADS * 2 * DIFF_DH
_UF0 = _VD0 + HEADS * DIFF_DV

LANE = 128
HW = HEADS * LANE
TM = 256
TQ = 256
FF_CHUNK = 256
N_FF_CHUNKS = D_FF // FF_CHUNK
GROUPS = TM // 8
HALO = 16
VMEM_LIMIT = 56 * 1024 * 1024


def _cparams(sem):
    return pltpu.CompilerParams(dimension_semantics=sem, vmem_limit_bytes=VMEM_LIMIT)


def _w1_columns(ctx):
    z = lambda n: [-1] * n
    r = lambda a, n: list(range(a, a + n))
    cols = []
    cols += r(_CKV0, MLA_RANK)
    cols += z(MLA_NOPE) + r(_KR0, MLA_ROPE) + z(LANE - MLA_NOPE - MLA_ROPE)
    for h in range(HEADS):
        b = _QM0 + h * (MLA_NOPE + MLA_ROPE)
        cols += r(b, MLA_NOPE) + r(b + MLA_NOPE, MLA_ROPE) + r(b + MLA_NOPE, MLA_ROPE)
    for h in range(HEADS):
        b = _QD0 + h * 2 * DIFF_DH
        cols += r(b, DIFF_DH) * 2 + r(b + DIFF_DH, DIFF_DH) * 2
    for h in range(HEADS):
        b = _KD0 + h * 2 * DIFF_DH
        cols += r(b, DIFF_DH) + z(DIFF_DH) + r(b + DIFF_DH, DIFF_DH) + z(DIFF_DH)
    for h in range(HEADS):
        cols += z(LANE - DIFF_DV) + r(_VD0 + h * DIFF_DV, DIFF_DV)
    cols += r(_UF0, F_WIDTH)
    if ctx:
        cols += r(_KD0, HEADS * 2 * DIFF_DH) + r(_VD0, HEADS * DIFF_DV)
        cols += r(_KR0, MLA_ROPE) + z(LANE - MLA_ROPE)
    return np.asarray(cols, np.int32)


_W1_COLS = {False: _w1_columns(False), True: _w1_columns(True)}
_C_CKV, _C_KR, _C_QM = 0, LANE, 2 * LANE
_C_QD = _C_QM + HW
_C_KD = _C_QD + HW
_C_VD = _C_KD + HW
_C_UF = _C_VD + HW
_C_XKD = _C_UF + F_WIDTH
_C_XVD = _C_XKD + HEADS * 2 * DIFF_DH
_C_XKR = _C_XVD + HEADS * DIFF_DV


def _w2_columns():
    cols = []
    for base in (0, HEADS * MLA_NOPE):
        for h in range(HEADS):
            cols += list(range(base + h * 64, base + (h + 1) * 64)) + [-1] * 64
    return np.asarray(cols, np.int32)


_W2_COLS = _w2_columns()


def _wout_rows():
    rows = []
    for h in range(HEADS):
        rows += list(range(h * MLA_DV, (h + 1) * MLA_DV))
        rows += list(range(HEADS * MLA_DV + h * DIFF_DV, HEADS * MLA_DV + (h + 1) * DIFF_DV))
    return np.asarray(rows, np.int32)


_WOUT_ROWS = _wout_rows()


def _take_cols(w, cols):
    g = jnp.take(w, jnp.asarray(np.maximum(cols, 0)), axis=-1)
    return g * jnp.asarray((cols >= 0).astype(np.float32))


def _selection(src_of_col, n_src):
    e = np.zeros((n_src, len(src_of_col)), np.float32)
    for c, s in enumerate(src_of_col):
        if s >= 0:
            e[s, c] = 1.0
    return e


def _cache_selections():
    kr, kd, vd = [], [], []
    for h in range(HEADS):
        kr += [-1] * (MLA_NOPE + MLA_ROPE) + list(range(MLA_ROPE))
        b = h * 2 * DIFF_DH
        kd += [-1] * DIFF_DH + list(range(b, b + DIFF_DH)) + [-1] * DIFF_DH + list(range(b + DIFF_DH, b + 2 * DIFF_DH))
        vd += [-1] * (LANE - DIFF_DV) + list(range(h * DIFF_DV, (h + 1) * DIFF_DV))
    return (_selection(kr, MLA_ROPE), _selection(kd, HEADS * 2 * DIFF_DH), _selection(vd, HEADS * DIFF_DV))


_E_KR, _E_KD, _E_VD = _cache_selections()


def _rope_tables(n_lat):
    t = np.arange(n_lat)
    row = (t // GRID_W).astype(np.float64)
    col = (t % GRID_W).astype(np.float64)
    quarter = MLA_ROPE // 4
    inv = ROPE_BASE ** (-np.arange(quarter, dtype=np.float64) / quarter)
    inv = inv.astype(np.float32).astype(np.float64)
    ang = np.concatenate([row[:, None] * inv, row[:, None] * inv, col[:, None] * inv, col[:, None] * inv], axis=1)
    ang = ang.astype(np.float32)
    cos32 = np.cos(ang).astype(np.float32)
    sign = np.concatenate([-np.ones(quarter), np.ones(quarter)] * 2).astype(np.float32)
    sin32 = np.sin(ang).astype(np.float32) * sign
    one, zero = np.ones((n_lat, 32), np.float32), np.zeros((n_lat, 32), np.float32)
    cm = np.concatenate([one, one, cos32, one], axis=1)
    sm = np.concatenate([zero, zero, sin32, zero], axis=1)
    cd = np.concatenate([cos32, one, cos32, one], axis=1)
    sd = np.concatenate([sin32, zero, sin32, zero], axis=1)
    return cm, sm, cd, sd


def _mod_kernel(cond_ref, w_ref, b_ref, q1_ref, k1_ref, q2_ref, k2_ref, li_ref, mod_ref, lam_ref):
    a = cond_ref[...]
    a = a * jax.nn.sigmoid(a)
    acc = jnp.dot(a.astype(BF16), w_ref[...].astype(BF16), preferred_element_type=F32)
    mod_ref[...] = acc + b_ref[...]
    d1 = jnp.sum(q1_ref[...] * k1_ref[...], axis=-1, keepdims=True)
    d2 = jnp.sum(q2_ref[...] * k2_ref[...], axis=-1, keepdims=True)
    lam_ref[...] = jnp.exp(d1) - jnp.exp(d2) + li_ref[...]


def _modulation_call(cond, w_ada, b_ada, lam_q1, lam_k1, lam_q2, lam_k2, lam_init_rows):
    depth = w_ada.shape[0]
    n_sec = w_ada.shape[2] // D_MODEL
    rows = cond.shape[0]
    vec = lambda a: a.reshape(depth, 1, a.shape[-1])
    lam_spec = pl.BlockSpec((None, 1, DIFF_DH), lambda l, s: (l, 0, 0))
    return pl.pallas_call(
        _mod_kernel,
        grid=(depth, n_sec),
        in_specs=[
            pl.BlockSpec((rows, D_MODEL), lambda l, s: (0, 0)),
            pl.BlockSpec((None, D_MODEL, D_MODEL), lambda l, s: (l, 0, s)),
            pl.BlockSpec((None, 1, D_MODEL), lambda l, s: (l, 0, s)),
            lam_spec, lam_spec, lam_spec, lam_spec,
            pl.BlockSpec((None, 1, LANE), lambda l, s: (l, 0, 0)),
        ],
        out_specs=[
            pl.BlockSpec((None, rows, D_MODEL), lambda l, s: (l, 0, s)),
            pl.BlockSpec((None, 1, LANE), lambda l, s: (l, 0, 0)),
        ],
        out_shape=[
            jax.ShapeDtypeStruct((depth, rows, n_sec * D_MODEL), F32),
            jax.ShapeDtypeStruct((depth, 1, LANE), F32),
        ],
        compiler_params=_cparams(("arbitrary", "arbitrary")),
        name="modulation",
    )(cond, w_ada, vec(b_ada), vec(lam_q1), vec(lam_k1), vec(lam_q2), vec(lam_k2), lam_init_rows)


def _rms(x):
    return x * lax.rsqrt(jnp.mean(x * x, axis=-1, keepdims=True) + EPS)


def _swap_halves(x):
    lane = lax.broadcasted_iota(jnp.int32, x.shape, 1)
    first = (lane & 8) == 0
    return jnp.where(first, pltpu.roll(x, LANE - 8, axis=1), pltpu.roll(x, 8, axis=1))


def _rope(x, c, s):
    return x * c + _swap_halves(x) * s


def _store_lane_blocks(ref, x):
    for k in range(ref.shape[0]):
        ref[k] = x[:, k * LANE:(k + 1) * LANE]


def _strided_rows(ref, start_of_group, stride):
    return jnp.concatenate(
        [jnp.concatenate([ref[k, pl.ds(start_of_group(j), 8, stride=stride), :] for k in range(ref.shape[0])], axis=1)
         for j in range(GROUPS)], axis=0)


def _inproj_kernel(*refs, rope, ctx, tiles_per_seq):
    it = iter(refs)
    l_ref = next(it)
    x_ref, shift_ref, scale_ref, g_ref, w1_ref, gkv_ref, w2_ref, cs_ref = (next(it) for _ in range(8))
    if rope:
        cm_ref, sm_ref, cd_ref, sd_ref = (next(it) for _ in range(4))
    qm_ref, km_ref, qd_ref, kd_ref, v_ref, ab_ref = (next(it) for _ in range(6))
    if ctx:
        ckv_o, kr_o, kdc_o, vdc_o = (next(it) for _ in range(4))
    h_scr = next(it)
    del l_ref

    x = x_ref[...]
    h = _rms(x) * g_ref[...] * (1.0 + scale_ref[...]) + shift_ref[...]
    h_scr[...] = h.astype(BF16)

    if rope:
        r0 = pl.multiple_of((pl.program_id(0) % tiles_per_seq) * TM, TM)
        cm, sm = cm_ref[pl.ds(r0, TM), :], sm_ref[pl.ds(r0, TM), :]
        cd, sd = cd_ref[pl.ds(r0, TM), :], sd_ref[pl.ds(r0, TM), :]

    def proj(c0, width):
        return jnp.dot(h_scr[...], w1_ref[:, c0:c0 + width], preferred_element_type=F32)

    acc = proj(_C_CKV, 2 * LANE)
    ckvn = _rms(acc[:, :LANE]) * gkv_ref[...]
    kr = acc[:, LANE:]
    if ctx:
        ckv_o[...] = ckvn
    if rope:
        kr = _rope(kr, cm, sm)
    kv2 = jnp.dot(ckvn.astype(BF16), w2_ref[...], preferred_element_type=F32)
    for hd in range(HEADS):
        km_ref[:, hd * LANE:(hd + 1) * LANE] = (kv2[:, hd * LANE:(hd + 1) * LANE] + kr).astype(BF16)

    def blocked(c_base, out_ref, c_tab, s_tab, scale, add=None):
        for j in range(HW // (2 * LANE)):
            acc = proj(c_base + j * 2 * LANE, 2 * LANE)
            for k in range(2):
                blk = acc[:, k * LANE:(k + 1) * LANE]
                col = (2 * j + k) * LANE
                if rope and c_tab is not None:
                    blk = _rope(blk, c_tab, s_tab)
                if scale is not None:
                    blk = blk * scale
                if add is not None:
                    blk = blk + add[:, col:col + LANE]
                out_ref[:, col:col + LANE] = blk.astype(BF16)

    blocked(_C_QM, qm_ref, cm if rope else None, sm if rope else None, MLA_SCALE)
    blocked(_C_QD, qd_ref, cd if rope else None, sd if rope else None, DIFF_SCALE)
    blocked(_C_KD, kd_ref, cd if rope else None, sd if rope else None, None)
    blocked(_C_VD, v_ref, None, None, None, add=kv2[:, HW:])

    uf = proj(_C_UF, F_WIDTH)
    ab_ref[...] = jnp.dot(uf.astype(BF16), cs_ref[...], preferred_element_type=F32).astype(BF16)

    if ctx:
        kdc_o[...] = proj(_C_XKD, HEADS * 2 * DIFF_DH)
        vdc_o[...] = proj(_C_XVD, HEADS * DIFF_DV)
        kr_o[...] = proj(_C_XKR, LANE)[:, :MLA_ROPE]


def _inproj_call(l_arr, x, mod5, b_row, g_mix, w1, g_kv, w2, cs64, tables, seq_len, ctx):
    t = x.shape[0]
    n_tiles = t // TM
    tiles_per_seq = seq_len // TM
    rope = tables is not None
    n1 = w1.shape[-1]
    row = lambda w: pl.BlockSpec((TM, w), lambda i, l: (i, 0))
    modspec = lambda sec: pl.BlockSpec((None, None, None, 1, D_MODEL), lambda i, l: (l[0], b_row(i), sec, 0, 0))
    in_specs = [
        row(D_MODEL), modspec(0), modspec(1),
        pl.BlockSpec((None, 1, D_MODEL), lambda i, l: (l[0], 0, 0)),
        pl.BlockSpec((None, D_MODEL, n1), lambda i, l: (l[0], 0, 0)),
        pl.BlockSpec((None, 1, MLA_RANK), lambda i, l: (l[0], 0, 0)),
        pl.BlockSpec((None, MLA_RANK, 2 * HW), lambda i, l: (l[0], 0, 0)),
        pl.BlockSpec((F_WIDTH, 2 * F_WIDTH), lambda i, l: (0, 0)),
    ]
    args = [x, mod5, mod5, g_mix, w1, g_kv, w2, cs64]
    if rope:
        in_specs += [pl.BlockSpec((seq_len, LANE), lambda i, l: (0, 0))] * 4
        args += list(tables)
    out_specs = [row(HW)] * 5 + [row(2 * F_WIDTH)]
    out_shape = [jax.ShapeDtypeStruct((t, HW), BF16)] * 5 + [jax.ShapeDtypeStruct((t, 2 * F_WIDTH), BF16)]
    if ctx:
        widths = (MLA_RANK, MLA_ROPE, HEADS * 2 * DIFF_DH, HEADS * DIFF_DV)
        out_specs += [row(w) for w in widths]
        out_shape += [jax.ShapeDtypeStruct((t, w), F32) for w in widths]
    return pl.pallas_call(
        functools.partial(_inproj_kernel, rope=rope, ctx=ctx, tiles_per_seq=tiles_per_seq),
        grid_spec=pltpu.PrefetchScalarGridSpec(
            num_scalar_prefetch=1, grid=(n_tiles,), in_specs=in_specs, out_specs=out_specs,
            scratch_shapes=[pltpu.VMEM((TM, D_MODEL), BF16)]),
        out_shape=out_shape,
        compiler_params=_cparams(("arbitrary",)),
        name="inproj_ctx" if ctx else "inproj_lat",
    )(l_arr, *args)


def _cacheprep_kernel(ckv_ref, kr_ref, kd_ref, vd_ref, w2_ref, ekr_ref, ekd_ref, evd_ref, km_ref, kdo_ref, v_ref):
    dot = lambda a, b: jnp.dot(a.astype(BF16), b, preferred_element_type=F32)
    kv2 = dot(ckv_ref[...], w2_ref[...])
    km_ref[...] = (kv2[:, :HW] + dot(kr_ref[...], ekr_ref[...])).astype(BF16)
    kdo_ref[...] = dot(kd_ref[...], ekd_ref[...]).astype(BF16)
    v_ref[...] = (kv2[:, HW:] + dot(vd_ref[...], evd_ref[...])).astype(BF16)


def _cacheprep_call(cache_ckv, cache_kr, cache_kd, cache_vd, w2, ekr, ekd, evd):
    b, depth, p = cache_ckv.shape[:3]
    cspec = lambda w: pl.BlockSpec((None, None, p, w), lambda l, i: (i, l, 0, 0))
    const = lambda a: pl.BlockSpec(a.shape, lambda l, i: (0,) * a.ndim)
    ospec = pl.BlockSpec((None, None, p, HW), lambda l, i: (l, i, 0, 0))
    oshape = jax.ShapeDtypeStruct((depth, b, p, HW), BF16)
    return pl.pallas_call(
        _cacheprep_kernel,
        grid=(depth, b),
        in_specs=[cspec(MLA_RANK), cspec(MLA_ROPE), cspec(HEADS * 2 * DIFF_DH), cspec(HEADS * DIFF_DV),
                  pl.BlockSpec((None, MLA_RANK, 2 * HW), lambda l, i: (l, 0, 0)),
                  const(ekr), const(ekd), const(evd)],
        out_specs=[ospec] * 3,
        out_shape=[oshape] * 3,
        compiler_params=_cparams(("arbitrary", "arbitrary")),
        name="cacheprep",
    )(cache_ckv, cache_kr, cache_kd, cache_vd, w2, ekr, ekd, evd)


def _softmax_pv(q, k_refs, v_refs):
    nt = (((1,), (1,)), ((), ()))
    s = [lax.dot_general(q, k[...], nt, preferred_element_type=F32) for k in k_refs]
    m = functools.reduce(jnp.maximum, [jnp.max(si, axis=-1, keepdims=True) for si in s])
    p = [jnp.exp(si - m) for si in s]
    den = functools.reduce(jnp.add, [jnp.sum(pi, axis=-1, keepdims=True) for pi in p])
    o = functools.reduce(jnp.add, [jnp.dot(pi.astype(BF16), v[...], preferred_element_type=F32)
                                   for pi, v in zip(p, v_refs)])
    return o / den


def _attn_kernel(*refs, n_seg):
    it = iter(refs)
    l_ref = next(it)
    lam_ref, gsub_ref, sub_ref, qm_ref, qd_ref = (next(it) for _ in range(5))
    km = [next(it) for _ in range(n_seg)]
    kd = [next(it) for _ in range(n_seg)]
    vv = [next(it) for _ in range(n_seg)]
    o_ref = next(it)
    del l_ref

    lane = lax.broadcasted_iota(jnp.int32, (TQ, LANE), 1)
    upper = lane >= LANE // 2
    o_m = _softmax_pv(qm_ref[...], km, vv)
    qd = qd_ref[...]
    zero = jnp.zeros_like(qd)
    o_1 = _softmax_pv(jnp.where(upper, zero, qd), kd, vv)
    o_2 = _softmax_pv(jnp.where(upper, qd, zero), kd, vv)
    o_d = jnp.where(upper, o_1 - lam_ref[...] * o_2, 0.0)
    ms = jnp.sum(o_d * o_d, axis=-1, keepdims=True) * (1.0 / DIFF_DV)
    o_d = o_d * lax.rsqrt(ms + EPS) * gsub_ref[...] * sub_ref[...]
    o_ref[...] = jnp.where(upper, o_d, o_m).astype(BF16)


def _attn_call(l_arr, lam, gsub, sub, qm, qd, segs, batch, n_q):
    n_seg = len(segs)
    q3 = lambda a: a.reshape(batch, n_q, HW)
    layer_vec = pl.BlockSpec((None, 1, LANE), lambda b, h, i, l: (l[0], 0, 0))
    qspec = pl.BlockSpec((None, TQ, LANE), lambda b, h, i, l: (b, i, h))
    kspecs, kargs = [], []
    for which in range(3):
        for seg in segs:
            a = seg[which]
            if a.ndim == 4:
                kspecs.append(pl.BlockSpec((None, None, a.shape[2], LANE), lambda b, h, i, l: (l[0], b, 0, h)))
            else:
                kspecs.append(pl.BlockSpec((None, a.shape[1], LANE), lambda b, h, i, l: (b, 0, h)))
            kargs.append(a)
    return pl.pallas_call(
        functools.partial(_attn_kernel, n_seg=n_seg),
        grid_spec=pltpu.PrefetchScalarGridSpec(
            num_scalar_prefetch=1, grid=(batch, HEADS, n_q // TQ),
            in_specs=[layer_vec, layer_vec, layer_vec, qspec, qspec] + kspecs,
            out_specs=qspec),
        out_shape=jax.ShapeDtypeStruct((batch, n_q, HW), BF16),
        compiler_params=_cparams(("arbitrary", "arbitrary", "arbitrary")),
        name="attention_%dseg" % n_seg,
    )(l_arr, lam, gsub, sub, q3(qm), q3(qd), *kargs).reshape(batch * n_q, HW)


def _dft_kernel(c_ref, s_ref, ab_ref, y_ref):
    ya = jnp.dot(c_ref[...], ab_ref[:, :F_WIDTH], preferred_element_type=F32)
    yb = jnp.dot(s_ref[...], ab_ref[:, F_WIDTH:], preferred_element_type=F32)
    y_ref[...] = (ya - yb).astype(BF16)


def _dft_call(cn, sn, ab, batch, n):
    tr = min(n, TM)
    mat = pl.BlockSpec((tr, n), lambda b, r: (r, 0))
    return pl.pallas_call(
        _dft_kernel,
        grid=(batch, n // tr),
        in_specs=[mat, mat, pl.BlockSpec((None, n, 2 * F_WIDTH), lambda b, r: (b, 0, 0))],
        out_specs=pl.BlockSpec((None, tr, F_WIDTH), lambda b, r: (b, r, 0)),
        out_shape=jax.ShapeDtypeStruct((batch, n, F_WIDTH), BF16),
        compiler_params=_cparams(("arbitrary", "arbitrary")),
        name="position_dft",
    )(cn, sn, ab.reshape(batch, n, 2 * F_WIDTH)).reshape(batch * n, F_WIDTH)


def _outproj_kernel(l_ref, o_ref, y_ref, x_ref, gate_ref, shift_ref, scale_ref, g_ref, wa_ref, wf_ref,
                    xo_ref, h_ref, hs_ref):
    del l_ref
    o = jnp.dot(o_ref[...], wa_ref[...], preferred_element_type=F32)
    o = o + jnp.dot(y_ref[...], wf_ref[...], preferred_element_type=F32)
    x = x_ref[...] + gate_ref[...] * o
    xo_ref[...] = x
    _store_lane_blocks(hs_ref, _rms(x) * g_ref[...] * (1.0 + scale_ref[...]) + shift_ref[...])
    h_ref[...] = _strided_rows(hs_ref, lambda g: g, GROUPS).astype(BF16)


def _outproj_call(l_arr, o, y, x, mod5, b_row, g_ffn, wo_a, wo_f):
    t = x.shape[0]
    row = lambda w: pl.BlockSpec((TM, w), lambda i, l: (i, 0))
    modspec = lambda sec: pl.BlockSpec((None, None, None, 1, D_MODEL), lambda i, l: (l[0], b_row(i), sec, 0, 0))
    return pl.pallas_call(
        _outproj_kernel,
        grid_spec=pltpu.PrefetchScalarGridSpec(
            num_scalar_prefetch=1, grid=(t // TM,),
            in_specs=[row(HW), row(F_WIDTH), row(D_MODEL), modspec(2), modspec(3), modspec(4),
                      pl.BlockSpec((None, 1, D_MODEL), lambda i, l: (l[0], 0, 0)),
                      pl.BlockSpec((None, HW, D_MODEL), lambda i, l: (l[0], 0, 0)),
                      pl.BlockSpec((None, F_WIDTH, D_MODEL), lambda i, l: (l[0], 0, 0))],
            out_specs=[row(D_MODEL), row(D_MODEL)],
            scratch_shapes=[pltpu.VMEM((D_MODEL // LANE, TM, LANE), F32)]),
        out_shape=[jax.ShapeDtypeStruct((t, D_MODEL), F32), jax.ShapeDtypeStruct((t, D_MODEL), BF16)],
        compiler_params=_cparams(("arbitrary",)),
        name="outproj",
    )(l_arr, o, y, x, mod5, mod5, mod5, g_ffn, wo_a, wo_f)


def _ffn_kernel(l_ref, h_ref, hp_ref, hn_ref, x_ref, gate_ref, gfin_ref, wup_ref, wconv_ref, wdn_ref,
                xo_ref, hx_ref, act_ref, ys_ref, *, tiles_per_seq, final):
    del l_ref
    i = pl.program_id(0)
    has_prev = (i % tiles_per_seq) != 0
    has_next = (i % tiles_per_seq) != tiles_per_seq - 1
    hx_ref[0:HALO, :] = jnp.where(has_prev, hp_ref[...], jnp.zeros_like(hp_ref))
    hx_ref[HALO:HALO + TM, :] = h_ref[...]
    hx_ref[HALO + TM:, :] = jnp.where(has_next, hn_ref[...], jnp.zeros_like(hn_ref))

    for c in range(N_FF_CHUNKS):
        u = jnp.dot(hx_ref[...], wup_ref[c], preferred_element_type=F32)
        wc = wconv_ref[c]
        mid = u[HALO:HALO + TM]
        wrap_dn = jnp.concatenate([u[HALO - 1:HALO], mid[TM - 8:TM - 1]], axis=0)
        wrap_up = jnp.concatenate([mid[1:8], u[HALO + TM:HALO + TM + 1]], axis=0)
        dn = jnp.concatenate([wrap_dn, mid[:TM - 8]], axis=0)
        up = jnp.concatenate([mid[8:], wrap_up], axis=0)
        u = dn * wc[0:1, :] + mid * wc[1:2, :] + up * wc[2:3, :]
        gate, val = u[:, :FF_CHUNK], u[:, FF_CHUNK:]
        act_ref[:, c * FF_CHUNK:(c + 1) * FF_CHUNK] = (gate * jax.nn.sigmoid(gate) * val).astype(BF16)

    _store_lane_blocks(ys_ref, jnp.dot(act_ref[...], wdn_ref[...], preferred_element_type=F32))
    y = _strided_rows(ys_ref, lambda j: 8 * ((8 * j) % GROUPS) + (8 * j) // GROUPS, 8)
    x = x_ref[...] + gate_ref[...] * y
    if final:
        x = _rms(x) * gfin_ref[...]
    xo_ref[...] = x


def _ffn_call(l_arr, h, x, mod5, b_row, g_final, wup, wconv, wdn, seq_len, final):
    t = x.shape[0]
    n_tiles = t // TM
    per = TM // HALO
    row = lambda w: pl.BlockSpec((TM, w), lambda i, l: (i, 0))
    whole = lambda a: pl.BlockSpec((None,) + a.shape[1:], lambda i, l: (l[0],) + (0,) * (a.ndim - 1))
    return pl.pallas_call(
        functools.partial(_ffn_kernel, tiles_per_seq=seq_len // TM, final=final),
        grid_spec=pltpu.PrefetchScalarGridSpec(
            num_scalar_prefetch=1, grid=(n_tiles,),
            in_specs=[row(D_MODEL),
                      pl.BlockSpec((HALO, D_MODEL), lambda i, l: (jnp.maximum(i * per - 1, 0), 0)),
                      pl.BlockSpec((HALO, D_MODEL), lambda i, l: (jnp.minimum((i + 1) * per, n_tiles * per - 1), 0)),
                      row(D_MODEL),
                      pl.BlockSpec((None, None, None, 1, D_MODEL), lambda i, l: (l[0], b_row(i), 5, 0, 0)),
                      pl.BlockSpec((1, D_MODEL), lambda i, l: (0, 0)),
                      whole(wup), whole(wconv), whole(wdn)],
            out_specs=row(D_MODEL),
            scratch_shapes=[pltpu.VMEM((TM + 2 * HALO, D_MODEL), BF16), pltpu.VMEM((TM, D_FF), BF16),
                            pltpu.VMEM((D_MODEL // LANE, TM, LANE), F32)]),
        out_shape=jax.ShapeDtypeStruct((t, D_MODEL), F32),
        compiler_params=_cparams(("arbitrary",)),
        name="ffn_final" if final else "ffn",
    )(l_arr, h, h, h, x, mod5, g_final, wup, wconv, wdn)


def _dft_mats(n):
    k = jnp.arange(n, dtype=jnp.int32)
    ang = ((k[:, None] * k[None, :]) % n).astype(F32) * (2.0 * math.pi / n)
    s = 1.0 / math.sqrt(n)
    return (jnp.cos(ang) * s).astype(BF16), (jnp.sin(ang) * s).astype(BF16)


def _channel_dft():
    k = np.arange(F_GROUP_W)
    ang = 2.0 * np.pi * ((k[:, None] * k[None, :]) % F_GROUP_W) / F_GROUP_W
    eye = np.eye(F_GROUPS)
    s = 1.0 / math.sqrt(F_GROUP_W)
    return np.concatenate([np.kron(eye, np.cos(ang) * s), np.kron(eye, np.sin(ang) * s)], axis=1).astype(np.float32)


def kernel(x_prompt, x_sample, c, cache_mla_ckv, cache_mla_krope, cache_diff_k, cache_diff_v, c_ctx, w_ada, b_ada,
           g_mix_norm, g_ffn_norm, w_in, g_kv_norm, w_uk, w_uv, lam_q1, lam_k1, lam_q2, lam_k2, g_diff_subln,
           w_out, w_up, w_conv, w_down, g_final):
    batch_c, seq_c, _ = x_prompt.shape
    batch_l, seq_l, _ = x_sample.shape
    depth = w_in.shape[0]
    past = cache_mla_ckv.shape[2]
    assert seq_c % TM == 0 and seq_l % TM == 0 and seq_l % GRID_W == 0 and D_FF % FF_CHUNK == 0

    w1_c = _take_cols(w_in, _W1_COLS[True]).astype(BF16)
    w1_l = w1_c[:, :, :_C_XKD]
    w2 = _take_cols(jnp.concatenate([w_uk, w_uv], axis=-1), _W2_COLS).astype(BF16)
    wo = jnp.take(w_out, jnp.asarray(_WOUT_ROWS), axis=1)
    wo_a = wo.astype(BF16)
    wo_f = w_out[:, HEADS * (MLA_DV + DIFF_DV):, :].astype(BF16)
    wup = jnp.concatenate([w_up[:, :, :D_FF].reshape(depth, D_MODEL, N_FF_CHUNKS, FF_CHUNK),
                           w_up[:, :, D_FF:].reshape(depth, D_MODEL, N_FF_CHUNKS, FF_CHUNK)], axis=-1)
    wup = wup.transpose(0, 2, 1, 3).astype(BF16)
    wconv = jnp.concatenate([w_conv[:, :, :D_FF].reshape(depth, 3, N_FF_CHUNKS, FF_CHUNK),
                             w_conv[:, :, D_FF:].reshape(depth, 3, N_FF_CHUNKS, FF_CHUNK)], axis=-1)
    wconv = wconv.transpose(0, 2, 1, 3)
    wdn = w_down.astype(BF16)
    vec = lambda a: a.reshape(depth, 1, a.shape[-1])
    g_mix, g_ffn, g_kv = vec(g_mix_norm), vec(g_ffn_norm), vec(g_kv_norm)
    gsub = jnp.concatenate([jnp.ones_like(g_diff_subln), g_diff_subln], axis=-1).reshape(depth, 1, LANE)
    lam_init = np.asarray([0.8 - 0.6 * math.exp(-0.3 * l) for l in range(depth)], np.float32)
    lam_init_rows = jnp.asarray(np.broadcast_to(lam_init[:, None, None], (depth, 1, LANE)))
    sub_rows = jnp.asarray(np.broadcast_to((1.0 - lam_init)[:, None, None], (depth, 1, LANE)).astype(np.float32))
    g_fin = g_final.reshape(1, D_MODEL)
    cs64 = jnp.asarray(_channel_dft()).astype(BF16)
    tables = tuple(jnp.asarray(a) for a in _rope_tables(seq_l))
    dft_c, dft_l = _dft_mats(seq_c), _dft_mats(seq_l)

    n_rows = -(-(batch_l + 1) // 8) * 8
    cond = jnp.zeros((n_rows, D_MODEL), F32).at[:batch_l].set(c).at[batch_l].set(c_ctx)
    mod, lam = _modulation_call(cond, w_ada, b_ada, lam_q1, lam_k1, lam_q2, lam_k2, lam_init_rows)
    mod5 = mod.reshape(depth, n_rows, w_ada.shape[2] // D_MODEL, 1, D_MODEL)
    tiles_l = seq_l // TM
    row_l = lambda i: i // tiles_l
    row_c = lambda i: batch_l

    flat = lambda a: a.reshape(a.shape[0], a.shape[1], a.shape[2], -1)
    km_p, kd_p, v_p = _cacheprep_call(cache_mla_ckv, cache_mla_krope, flat(cache_diff_k), flat(cache_diff_v), w2,
                                      jnp.asarray(_E_KR).astype(BF16), jnp.asarray(_E_KD).astype(BF16),
                                      jnp.asarray(_E_VD).astype(BF16))

    x_c = x_prompt.reshape(batch_c * seq_c, D_MODEL)
    x_l = x_sample.reshape(batch_l * seq_l, D_MODEL)
    new_ckv, new_kr, new_kd, new_vd = [], [], [], []
    for l in range(depth):
        l_arr = jnp.full((1,), l, jnp.int32)
        final = l == depth - 1
        qm, km, qd, kd, v, ab, ckv_o, kr_o, kd_o, vd_o = _inproj_call(
            l_arr, x_c, mod5, row_c, g_mix, w1_c, g_kv, w2, cs64, None, seq_c, True)
        k3 = lambda a: a.reshape(batch_c, seq_c, HW)
        o = _attn_call(l_arr, lam, gsub, sub_rows, qm, qd, [(k3(km), k3(kd), k3(v))], batch_c, seq_c)
        y = _dft_call(*dft_c, ab, batch_c, seq_c)
        x_c, h2 = _outproj_call(l_arr, o, y, x_c, mod5, row_c, g_ffn, wo_a, wo_f)
        x_c = _ffn_call(l_arr, h2, x_c, mod5, row_c, g_fin, wup, wconv, wdn, seq_c, final)
        new_ckv.append(ckv_o)
        new_kr.append(kr_o)
        new_kd.append(kd_o)
        new_vd.append(vd_o)
        qm, km, qd, kd, v, ab = _inproj_call(
            l_arr, x_l, mod5, row_l, g_mix, w1_l, g_kv, w2, cs64, tables, seq_l, False)
        k3 = lambda a: a.reshape(batch_l, seq_l, HW)
        o = _attn_call(l_arr, lam, gsub, sub_rows, qm, qd, [(k3(km), k3(kd), k3(v)), (km_p, kd_p, v_p)],
                       batch_l, seq_l)
        y = _dft_call(*dft_l, ab, batch_l, seq_l)
        x_l, h2 = _outproj_call(l_arr, o, y, x_l, mod5, row_l, g_ffn, wo_a, wo_f)
        x_l = _ffn_call(l_arr, h2, x_l, mod5, row_l, g_fin, wup, wconv, wdn, seq_l, final)

    stack = lambda parts, shape: jnp.stack([p.reshape((batch_c, seq_c) + shape) for p in parts], axis=1)
    return (x_c.reshape(batch_c, seq_c, D_MODEL),
            x_l.reshape(batch_l, seq_l, D_MODEL),
            stack(new_ckv, (MLA_RANK,)),
            stack(new_kr, (MLA_ROPE,)),
            stack(new_kd, (HEADS, 2 * DIFF_DH)),
            stack(new_vd, (HEADS, DIFF_DV)))
```

```python
import functools
import math

import numpy as np
import jax
import jax.numpy as jnp
from jax import lax
from jax.experimental import pallas as pl
from jax.experimental.pallas import tpu as pltpu

F32 = jnp.float32
BF16 = jnp.bfloat16

D_MODEL = 1024
GRID_W = 64
HEADS = 6
MLA_NOPE, MLA_ROPE, MLA_DV, MLA_RANK = 64, 32, 64, 128
DIFF_DH, DIFF_DV = 32, 64
F_GROUPS, F_GROUP_W = 4, 64
F_WIDTH = F_GROUPS * F_GROUP_W
D_FF = 2816
ROPE_BASE = 10000.0
EPS = 1e-6
MLA_SCALE = (MLA_NOPE + MLA_ROPE) ** -0.5
DIFF_SCALE = DIFF_DH ** -0.5
LOG2E = math.log2(math.e)

_QM0 = 0
_CKV0 = HEADS * (MLA_NOPE + MLA_ROPE)
_KR0 = _CKV0 + MLA_RANK
_QD0 = _KR0 + MLA_ROPE
_KD0 = _QD0 + HEADS * 2 * DIFF_DH
_VD0 = _KD0 + HEADS * 2 * DIFF_DH
_UF0 = _VD0 + HEADS * DIFF_DV

LANE = 128
HW = HEADS * LANE
TM = 256
TQ_CTX = 256
TQ_LAT = 1024
HEADS_LAT = 2
TQ_SUB = 512
FF_CHUNK = 256
N_FF_CHUNKS = D_FF // FF_CHUNK
GROUPS = TM // 8
HALO = 8
VMEM_LIMIT = 56 * 1024 * 1024


def _cparams(sem):
    return pltpu.CompilerParams(dimension_semantics=sem, vmem_limit_bytes=VMEM_LIMIT)


def _w1_columns(ctx):
    z = lambda n: [-1] * n
    r = lambda a, n: list(range(a, a + n))
    cols = []
    cols += r(_CKV0, MLA_RANK)
    cols += z(MLA_NOPE) + r(_KR0, MLA_ROPE) + z(LANE - MLA_NOPE - MLA_ROPE)
    for h in range(HEADS):
        b = _QM0 + h * (MLA_NOPE + MLA_ROPE)
        cols += r(b, MLA_NOPE) + r(b + MLA_NOPE, MLA_ROPE) + r(b + MLA_NOPE, MLA_ROPE)
    for h in range(HEADS):
        b = _QD0 + h * 2 * DIFF_DH
        cols += r(b, DIFF_DH) * 2 + r(b + DIFF_DH, DIFF_DH) * 2
    for h in range(HEADS):
        b = _KD0 + h * 2 * DIFF_DH
        cols += r(b, DIFF_DH) + z(DIFF_DH) + r(b + DIFF_DH, DIFF_DH) + z(DIFF_DH)
    for h in range(HEADS):
        cols += z(LANE - DIFF_DV) + r(_VD0 + h * DIFF_DV, DIFF_DV)
    cols += r(_UF0, F_WIDTH)
    if ctx:
        cols += r(_KD0, HEADS * 2 * DIFF_DH) + r(_VD0, HEADS * DIFF_DV)
        cols += r(_KR0, MLA_ROPE) + z(LANE - MLA_ROPE)
    return np.asarray(cols, np.int32)


_W1_COLS = {False: _w1_columns(False), True: _w1_columns(True)}
_C_CKV, _C_KR, _C_QM = 0, LANE, 2 * LANE
_C_QD = _C_QM + HW
_C_KD = _C_QD + HW
_C_VD = _C_KD + HW
_C_UF = _C_VD + HW
_C_XKD = _C_UF + F_WIDTH
_C_XVD = _C_XKD + HEADS * 2 * DIFF_DH
_C_XKR = _C_XVD + HEADS * DIFF_DV


def _w2_columns():
    cols = []
    for base in (0, HEADS * MLA_NOPE):
        for h in range(HEADS):
            cols += list(range(base + h * 64, base + (h + 1) * 64)) + [-1] * 64
    return np.asarray(cols, np.int32)


_W2_COLS = _w2_columns()


def _wout_rows():
    rows = []
    for h in range(HEADS):
        rows += list(range(h * MLA_DV, (h + 1) * MLA_DV))
        rows += list(range(HEADS * MLA_DV + h * DIFF_DV, HEADS * MLA_DV + (h + 1) * DIFF_DV))
    return np.asarray(rows, np.int32)


_WOUT_ROWS = _wout_rows()


def _ffn_columns():
    cols = []
    for c in range(N_FF_CHUNKS):
        cols += list(range(c * FF_CHUNK, (c + 1) * FF_CHUNK))
        cols += list(range(D_FF + c * FF_CHUNK, D_FF + (c + 1) * FF_CHUNK))
    return np.asarray(cols, np.int32)


_FFN_COLS = _ffn_columns()


def _take(w, idx, axis=-1):
    axis = axis % w.ndim
    pieces, i, n = [], 0, len(idx)
    while i < n:
        j = i + 1
        if idx[i] < 0:
            while j < n and idx[j] < 0:
                j += 1
            shape = w.shape[:axis] + (j - i,) + w.shape[axis + 1:]
            pieces.append(jnp.zeros(shape, w.dtype))
        else:
            while j < n and idx[j] == idx[j - 1] + 1:
                j += 1
            pieces.append(lax.slice_in_dim(w, int(idx[i]), int(idx[i]) + j - i, axis=axis))
        i = j
    return jnp.concatenate(pieces, axis=axis)


def _selection(src_of_col, n_src):
    e = np.zeros((n_src, len(src_of_col)), np.float32)
    for c, s in enumerate(src_of_col):
        if s >= 0:
            e[s, c] = 1.0
    return e


def _cache_selections():
    kr, kd, vd = [], [], []
    for h in range(HEADS):
        kr += [-1] * (MLA_NOPE + MLA_ROPE) + list(range(MLA_ROPE))
        b = h * 2 * DIFF_DH
        kd += [-1] * DIFF_DH + list(range(b, b + DIFF_DH)) + [-1] * DIFF_DH + list(range(b + DIFF_DH, b + 2 * DIFF_DH))
        vd += [-1] * (LANE - DIFF_DV) + list(range(h * DIFF_DV, (h + 1) * DIFF_DV))
    return (_selection(kr, MLA_ROPE), _selection(kd, HEADS * 2 * DIFF_DH), _selection(vd, HEADS * DIFF_DV))


_E_KR, _E_KD, _E_VD = _cache_selections()


def _rope_tables(n_lat):
    t = np.arange(n_lat)
    row = (t // GRID_W).astype(np.float64)
    col = (t % GRID_W).astype(np.float64)
    quarter = MLA_ROPE // 4
    inv = ROPE_BASE ** (-np.arange(quarter, dtype=np.float64) / quarter)
    inv = inv.astype(np.float32).astype(np.float64)
    ang = np.concatenate([row[:, None] * inv, row[:, None] * inv, col[:, None] * inv, col[:, None] * inv], axis=1)
    ang = ang.astype(np.float32)
    cos32 = np.cos(ang).astype(np.float32)
    sign = np.concatenate([-np.ones(quarter), np.ones(quarter)] * 2).astype(np.float32)
    sin32 = np.sin(ang).astype(np.float32) * sign
    one, zero = np.ones((n_lat, 32), np.float32), np.zeros((n_lat, 32), np.float32)
    cm = np.concatenate([one, one, cos32, one], axis=1)
    sm = np.concatenate([zero, zero, sin32, zero], axis=1)
    cd = np.concatenate([cos32, one, cos32, one], axis=1)
    sd = np.concatenate([sin32, zero, sin32, zero], axis=1)
    return cm, sm, cd, sd


def _mod_kernel(cond_ref, w_ref, b_ref, q1_ref, k1_ref, q2_ref, k2_ref, li_ref, mod_ref, lam_ref):
    a = cond_ref[...]
    a = a * jax.nn.sigmoid(a)
    acc = jnp.dot(a.astype(BF16), w_ref[...].astype(BF16), preferred_element_type=F32)
    mod_ref[...] = acc + b_ref[...]
    d1 = jnp.sum(q1_ref[...] * k1_ref[...], axis=-1, keepdims=True)
    d2 = jnp.sum(q2_ref[...] * k2_ref[...], axis=-1, keepdims=True)
    lam_ref[...] = jnp.exp(d1) - jnp.exp(d2) + li_ref[...]


def _modulation_call(cond, w_ada, b_ada, lam_q1, lam_k1, lam_q2, lam_k2, lam_init_rows):
    depth = w_ada.shape[0]
    n_sec = w_ada.shape[2] // D_MODEL
    rows = cond.shape[0]
    vec = lambda a: a.reshape(depth, 1, a.shape[-1])
    lam_spec = pl.BlockSpec((None, 1, DIFF_DH), lambda l, s: (l, 0, 0))
    return pl.pallas_call(
        _mod_kernel,
        grid=(depth, n_sec),
        in_specs=[
            pl.BlockSpec((rows, D_MODEL), lambda l, s: (0, 0)),
            pl.BlockSpec((None, D_MODEL, D_MODEL), lambda l, s: (l, 0, s)),
            pl.BlockSpec((None, 1, D_MODEL), lambda l, s: (l, 0, s)),
            lam_spec, lam_spec, lam_spec, lam_spec,
            pl.BlockSpec((None, 1, LANE), lambda l, s: (l, 0, 0)),
        ],
        out_specs=[
            pl.BlockSpec((None, rows, D_MODEL), lambda l, s: (l, 0, s)),
            pl.BlockSpec((None, 1, LANE), lambda l, s: (l, 0, 0)),
        ],
        out_shape=[
            jax.ShapeDtypeStruct((depth, rows, n_sec * D_MODEL), F32),
            jax.ShapeDtypeStruct((depth, 1, LANE), F32),
        ],
        compiler_params=_cparams(("arbitrary", "arbitrary")),
        name="modulation",
    )(cond, w_ada, vec(b_ada), vec(lam_q1), vec(lam_k1), vec(lam_q2), vec(lam_k2), lam_init_rows)


def _rms(x):
    return x * lax.rsqrt(jnp.mean(x * x, axis=-1, keepdims=True) + EPS)


def _swap_halves(x):
    lane = lax.broadcasted_iota(jnp.int32, x.shape, 1)
    first = (lane & 8) == 0
    return jnp.where(first, pltpu.roll(x, LANE - 8, axis=1), pltpu.roll(x, 8, axis=1))


def _rope(x, c, s):
    return x * c + _swap_halves(x) * s


def _store_lane_blocks(ref, x):
    for k in range(ref.shape[0]):
        ref[k] = x[:, k * LANE:(k + 1) * LANE]


def _strided_rows(ref, start_of_group, stride):
    return jnp.concatenate(
        [jnp.concatenate([ref[k, pl.ds(start_of_group(j), 8, stride=stride), :] for k in range(ref.shape[0])], axis=1)
         for j in range(GROUPS)], axis=0)


def _inproj_kernel(*refs, rope, ctx, tiles_per_seq):
    it = iter(refs)
    l_ref = next(it)
    x_ref, shift_ref, scale_ref, g_ref, w1_ref, gkv_ref, w2_ref, cs_ref = (next(it) for _ in range(8))
    if rope:
        cm_ref, sm_ref, cd_ref, sd_ref = (next(it) for _ in range(4))
    if ctx:
        for _ in range(4):
            next(it)
    qm_ref, km_ref, qd_ref, kd_ref, vm_ref, vd_ref, ab_ref = (next(it) for _ in range(7))
    if ctx:
        ckv_o, kr_o, kdc_o, vdc_o = (next(it) for _ in range(4))
    h_scr = next(it)
    del l_ref
    lane = lax.broadcasted_iota(jnp.int32, (TM, LANE), 1)
    ones_hi = (lane >= LANE // 2).astype(F32)
    ones_lo = 1.0 - ones_hi

    x = x_ref[...]
    h = _rms(x) * g_ref[...] * (1.0 + scale_ref[...]) + shift_ref[...]
    h_scr[...] = h.astype(BF16)

    if rope:
        r0 = pl.multiple_of((pl.program_id(0) % tiles_per_seq) * TM, TM)
        cm, sm = cm_ref[pl.ds(r0, TM), :], sm_ref[pl.ds(r0, TM), :]
        cd, sd = cd_ref[pl.ds(r0, TM), :], sd_ref[pl.ds(r0, TM), :]

    def proj(c0, width):
        return jnp.dot(h_scr[...], w1_ref[:, c0:c0 + width], preferred_element_type=F32)

    acc = proj(_C_CKV, 2 * LANE)
    ckvn = _rms(acc[:, :LANE]) * gkv_ref[...]
    kr = acc[:, LANE:]
    if ctx:
        ckv_o[...] = ckvn
    if rope:
        kr = _rope(kr, cm, sm)
    kv2 = jnp.dot(ckvn.astype(BF16), w2_ref[...], preferred_element_type=F32)
    for hd in range(HEADS):
        blk = slice(hd * LANE, (hd + 1) * LANE)
        km_ref[:, blk] = (kv2[:, blk] + kr).astype(BF16)
        vm_ref[:, blk] = (kv2[:, HW + hd * LANE:HW + (hd + 1) * LANE] + ones_hi).astype(BF16)

    def blocked(c_base, out_ref, c_tab, s_tab, scale, add=None):
        for j in range(HW // (2 * LANE)):
            acc = proj(c_base + j * 2 * LANE, 2 * LANE)
            for k in range(2):
                blk = acc[:, k * LANE:(k + 1) * LANE]
                col = (2 * j + k) * LANE
                if rope and c_tab is not None:
                    blk = _rope(blk, c_tab, s_tab)
                if scale is not None:
                    blk = blk * scale
                if add is not None:
                    blk = blk + add
                out_ref[:, col:col + LANE] = blk.astype(BF16)

    blocked(_C_QM, qm_ref, cm if rope else None, sm if rope else None, MLA_SCALE * LOG2E)
    blocked(_C_QD, qd_ref, cd if rope else None, sd if rope else None, DIFF_SCALE * LOG2E)
    blocked(_C_KD, kd_ref, cd if rope else None, sd if rope else None, None)
    blocked(_C_VD, vd_ref, None, None, None, add=ones_lo)

    uf = proj(_C_UF, F_WIDTH)
    ab_ref[...] = jnp.dot(uf.astype(BF16), cs_ref[...], preferred_element_type=F32).astype(BF16)

    if ctx:
        kdc_o[...] = proj(_C_XKD, HEADS * 2 * DIFF_DH)
        vdc_o[...] = proj(_C_XVD, HEADS * DIFF_DV)
        kr_o[...] = proj(_C_XKR, LANE)[:, :MLA_ROPE]


def _inproj_call(l_arr, x, mod5, b_row, g_mix, w1, g_kv, w2, cs64, tables, seq_len, caches):
    t = x.shape[0]
    n_tiles = t // TM
    tiles_per_seq = seq_len // TM
    rope = tables is not None
    ctx = caches is not None
    n1 = _C_XKR + LANE if ctx else _C_XKD
    row = lambda w: pl.BlockSpec((TM, w), lambda i, l: (i, 0))
    modspec = lambda sec: pl.BlockSpec((None, None, None, 1, D_MODEL), lambda i, l: (l[0], b_row(i), sec, 0, 0))
    in_specs = [
        row(D_MODEL), modspec(0), modspec(1),
        pl.BlockSpec((None, 1, D_MODEL), lambda i, l: (l[0], 0, 0)),
        pl.BlockSpec((None, D_MODEL, n1), lambda i, l: (l[0], 0, 0)),
        pl.BlockSpec((None, 1, MLA_RANK), lambda i, l: (l[0], 0, 0)),
        pl.BlockSpec((None, MLA_RANK, 2 * HW), lambda i, l: (l[0], 0, 0)),
        pl.BlockSpec((F_WIDTH, 2 * F_WIDTH), lambda i, l: (0, 0)),
    ]
    args = [x, mod5, mod5, g_mix, w1, g_kv, w2, cs64]
    if rope:
        in_specs += [pl.BlockSpec((seq_len, LANE), lambda i, l: (0, 0))] * 4
        args += list(tables)
    out_specs = [row(HW)] * 6 + [row(2 * F_WIDTH)]
    out_shape = [jax.ShapeDtypeStruct((t, HW), BF16)] * 6 + [jax.ShapeDtypeStruct((t, 2 * F_WIDTH), BF16)]
    aliases = {}
    if ctx:
        for k, a in enumerate(caches):
            aliases[1 + len(args)] = len(out_shape)
            in_specs.append(pl.BlockSpec(memory_space=pl.ANY))
            args.append(a)
            out_specs.append(pl.BlockSpec((None, None, TM, a.shape[-1]),
                                          lambda i, l: (i // tiles_per_seq, l[0], i % tiles_per_seq, 0)))
            out_shape.append(jax.ShapeDtypeStruct(a.shape, a.dtype))
    return pl.pallas_call(
        functools.partial(_inproj_kernel, rope=rope, ctx=ctx, tiles_per_seq=tiles_per_seq),
        grid_spec=pltpu.PrefetchScalarGridSpec(
            num_scalar_prefetch=1, grid=(n_tiles,), in_specs=in_specs, out_specs=out_specs,
            scratch_shapes=[pltpu.VMEM((TM, D_MODEL), BF16)]),
        out_shape=out_shape,
        input_output_aliases=aliases,
        compiler_params=_cparams(("arbitrary",)),
        name="inproj_ctx" if ctx else "inproj_lat",
    )(l_arr, *args)


def _cacheprep_kernel(ckv_ref, kr_ref, kd_ref, vd_ref, w2_ref, ekr_ref, ekd_ref, evd_ref,
                      km_ref, kdo_ref, vm_ref, vdo_ref):
    dot = lambda a, b: jnp.dot(a.astype(BF16), b, preferred_element_type=F32)
    kv2 = dot(ckv_ref[...], w2_ref[...])
    ones_hi = ((lax.broadcasted_iota(jnp.int32, (1, HW), 1) % LANE) >= LANE // 2).astype(F32)
    km_ref[...] = (kv2[:, :HW] + dot(kr_ref[...], ekr_ref[...])).astype(BF16)
    kdo_ref[...] = dot(kd_ref[...], ekd_ref[...]).astype(BF16)
    vm_ref[...] = (kv2[:, HW:] + ones_hi).astype(BF16)
    vdo_ref[...] = (dot(vd_ref[...], evd_ref[...]) + (1.0 - ones_hi)).astype(BF16)


def _cacheprep_call(cache_ckv, cache_kr, cache_kd, cache_vd, w2, ekr, ekd, evd):
    b, depth, p = cache_ckv.shape[:3]
    cspec = lambda w: pl.BlockSpec((None, None, p, w), lambda l, i: (i, l, 0, 0))
    const = lambda a: pl.BlockSpec(a.shape, lambda l, i: (0,) * a.ndim)
    ospec = pl.BlockSpec((None, None, p, HW), lambda l, i: (l, i, 0, 0))
    oshape = jax.ShapeDtypeStruct((depth, b, p, HW), BF16)
    return pl.pallas_call(
        _cacheprep_kernel,
        grid=(depth, b),
        in_specs=[cspec(MLA_RANK), cspec(MLA_ROPE), cspec(HEADS * 2 * DIFF_DH), cspec(HEADS * DIFF_DV),
                  pl.BlockSpec((None, MLA_RANK, 2 * HW), lambda l, i: (l, 0, 0)),
                  const(ekr), const(ekd), const(evd)],
        out_specs=[ospec] * 4,
        out_shape=[oshape] * 4,
        compiler_params=_cparams(("arbitrary", "arbitrary")),
        name="cacheprep",
    )(cache_ckv, cache_kr, cache_kd, cache_vd, w2, ekr, ekd, evd)


def _softmax_pv(q, k_refs, v_refs, blk):
    nt = (((1,), (1,)), ((), ()))
    s = [lax.dot_general(q, k[:, blk], nt, preferred_element_type=F32) for k in k_refs]
    m = functools.reduce(jnp.maximum, [jnp.max(si, axis=-1, keepdims=True) for si in s])
    o = functools.reduce(jnp.add, [jnp.dot(jnp.exp2(si - m).astype(BF16), v[:, blk], preferred_element_type=F32)
                                   for si, v in zip(s, v_refs)])
    return o / pltpu.roll(o, LANE // 2, axis=1)


def _attn_kernel(*refs, n_seg, heads, tq):
    it = iter(refs)
    l_ref = next(it)
    lam_ref, gsub_ref, sub_ref, qm_ref, qd_ref = (next(it) for _ in range(5))
    km = [next(it) for _ in range(n_seg)]
    kd = [next(it) for _ in range(n_seg)]
    vm = [next(it) for _ in range(n_seg)]
    vd = [next(it) for _ in range(n_seg)]
    o_ref = next(it)
    del l_ref

    sub = min(tq, TQ_SUB)
    upper = lax.broadcasted_iota(jnp.int32, (sub, LANE), 1) >= LANE // 2
    for hd in range(heads):
        blk = slice(hd * LANE, (hd + 1) * LANE)
        for r in range(tq // sub):
            rows = slice(r * sub, (r + 1) * sub)
            o_m = _softmax_pv(qm_ref[rows, blk], km, vm, blk)
            qd = qd_ref[rows, blk]
            zero = jnp.zeros_like(qd)
            o_1 = _softmax_pv(jnp.where(upper, zero, qd), kd, vd, blk)
            o_2 = _softmax_pv(jnp.where(upper, qd, zero), kd, vd, blk)
            o_d = jnp.where(upper, o_1 - lam_ref[...] * o_2, 0.0)
            ms = jnp.sum(o_d * o_d, axis=-1, keepdims=True) * (1.0 / DIFF_DV)
            o_d = o_d * lax.rsqrt(ms + EPS) * gsub_ref[...] * sub_ref[...]
            o_ref[rows, blk] = jnp.where(upper, o_d, o_m).astype(BF16)


def _attn_call(l_arr, lam, gsub, sub, qm, qd, segs, batch, n_q, tq, heads):
    n_seg = len(segs)
    width = heads * LANE
    q3 = lambda a: a.reshape(batch, n_q, HW)
    layer_vec = pl.BlockSpec((None, 1, LANE), lambda b, h, i, l: (l[0], 0, 0))
    qspec = pl.BlockSpec((None, tq, width), lambda b, h, i, l: (b, i, h))
    kspecs, kargs = [], []
    for which in range(4):
        for seg in segs:
            a = seg[which]
            if a.ndim == 4:
                kspecs.append(pl.BlockSpec((None, None, a.shape[2], width), lambda b, h, i, l: (l[0], b, 0, h)))
            else:
                kspecs.append(pl.BlockSpec((None, a.shape[1], width), lambda b, h, i, l: (b, 0, h)))
            kargs.append(a)
    return pl.pallas_call(
        functools.partial(_attn_kernel, n_seg=n_seg, heads=heads, tq=tq),
        grid_spec=pltpu.PrefetchScalarGridSpec(
            num_scalar_prefetch=1, grid=(batch, HEADS // heads, n_q // tq),
            in_specs=[layer_vec, layer_vec, layer_vec, qspec, qspec] + kspecs,
            out_specs=qspec),
        out_shape=jax.ShapeDtypeStruct((batch, n_q, HW), BF16),
        compiler_params=_cparams(("arbitrary", "arbitrary", "arbitrary")),
        name="attention_%dseg" % n_seg,
    )(l_arr, lam, gsub, sub, q3(qm), q3(qd), *kargs).reshape(batch * n_q, HW)


def _dft_kernel(c_ref, s_ref, ab_ref, y_ref):
    ya = jnp.dot(c_ref[...], ab_ref[:, :F_WIDTH], preferred_element_type=F32)
    yb = jnp.dot(s_ref[...], ab_ref[:, F_WIDTH:], preferred_element_type=F32)
    y_ref[...] = (ya - yb).astype(BF16)


def _dft_call(cn, sn, ab, batch, n):
    tr = min(n, TM)
    mat = pl.BlockSpec((tr, n), lambda b, r: (r, 0))
    return pl.pallas_call(
        _dft_kernel,
        grid=(batch, n // tr),
        in_specs=[mat, mat, pl.BlockSpec((None, n, 2 * F_WIDTH), lambda b, r: (b, 0, 0))],
        out_specs=pl.BlockSpec((None, tr, F_WIDTH), lambda b, r: (b, r, 0)),
        out_shape=jax.ShapeDtypeStruct((batch, n, F_WIDTH), BF16),
        compiler_params=_cparams(("arbitrary", "arbitrary")),
        name="position_dft",
    )(cn, sn, ab.reshape(batch, n, 2 * F_WIDTH)).reshape(batch * n, F_WIDTH)


def _outproj_kernel(l_ref, o_ref, y_ref, x_ref, gate_ref, shift_ref, scale_ref, g_ref, wa_ref, wf_ref,
                    xo_ref, h_ref, hs_ref):
    del l_ref
    o = jnp.dot(o_ref[...], wa_ref[...], preferred_element_type=F32)
    o = o + jnp.dot(y_ref[...], wf_ref[...], preferred_element_type=F32)
    x = x_ref[...] + gate_ref[...] * o
    xo_ref[...] = x
    _store_lane_blocks(hs_ref, _rms(x) * g_ref[...] * (1.0 + scale_ref[...]) + shift_ref[...])
    h_ref[...] = _strided_rows(hs_ref, lambda g: g, GROUPS).astype(BF16)


def _outproj_call(l_arr, o, y, x, mod5, b_row, g_ffn, wo_a, wo_f):
    t = x.shape[0]
    row = lambda w: pl.BlockSpec((TM, w), lambda i, l: (i, 0))
    modspec = lambda sec: pl.BlockSpec((None, None, None, 1, D_MODEL), lambda i, l: (l[0], b_row(i), sec, 0, 0))
    return pl.pallas_call(
        _outproj_kernel,
        grid_spec=pltpu.PrefetchScalarGridSpec(
            num_scalar_prefetch=1, grid=(t // TM,),
            in_specs=[row(HW), row(F_WIDTH), row(D_MODEL), modspec(2), modspec(3), modspec(4),
                      pl.BlockSpec((None, 1, D_MODEL), lambda i, l: (l[0], 0, 0)),
                      pl.BlockSpec((None, HW, D_MODEL), lambda i, l: (l[0], 0, 0)),
                      pl.BlockSpec((None, F_WIDTH, D_MODEL), lambda i, l: (l[0], 0, 0))],
            out_specs=[row(D_MODEL), row(D_MODEL)],
            scratch_shapes=[pltpu.VMEM((D_MODEL // LANE, TM, LANE), F32)]),
        out_shape=[jax.ShapeDtypeStruct((t, D_MODEL), F32), jax.ShapeDtypeStruct((t, D_MODEL), BF16)],
        compiler_params=_cparams(("arbitrary",)),
        name="outproj",
    )(l_arr, o, y, x, mod5, mod5, mod5, g_ffn, wo_a, wo_f)


def _ffn_kernel(l_ref, h_ref, hp_ref, hn_ref, x_ref, gate_ref, gfin_ref, wup_ref, wconv_ref, wdn_ref,
                xo_ref, hx_ref, act_ref, ys_ref, *, tiles_per_seq, final):
    del l_ref
    i = pl.program_id(0)
    has_prev = (i % tiles_per_seq) != 0
    has_next = (i % tiles_per_seq) != tiles_per_seq - 1
    hx_ref[0:TM, :] = h_ref[...]
    hx_ref[TM:TM + HALO, :] = jnp.where(has_prev, hp_ref[...], jnp.zeros_like(hp_ref))
    hx_ref[TM + HALO:, :] = jnp.where(has_next, hn_ref[...], jnp.zeros_like(hn_ref))

    for c in range(N_FF_CHUNKS):
        cols = slice(c * 2 * FF_CHUNK, (c + 1) * 2 * FF_CHUNK)
        u = jnp.dot(hx_ref[...], wup_ref[:, cols], preferred_element_type=F32)
        wc = wconv_ref[:, cols]
        mid = u[:TM]
        wrap_dn = jnp.concatenate([u[TM + HALO - 1:TM + HALO], mid[TM - 8:TM - 1]], axis=0)
        wrap_up = jnp.concatenate([mid[1:8], u[TM + HALO:TM + HALO + 1]], axis=0)
        dn = jnp.concatenate([wrap_dn, mid[:TM - 8]], axis=0)
        up = jnp.concatenate([mid[8:], wrap_up], axis=0)
        u = dn * wc[0:1, :] + mid * wc[1:2, :] + up * wc[2:3, :]
        gate, val = u[:, :FF_CHUNK], u[:, FF_CHUNK:]
        act_ref[:, c * FF_CHUNK:(c + 1) * FF_CHUNK] = (gate * jax.nn.sigmoid(gate) * val).astype(BF16)

    _store_lane_blocks(ys_ref, jnp.dot(act_ref[...], wdn_ref[...], preferred_element_type=F32))
    y = _strided_rows(ys_ref, lambda j: 8 * ((8 * j) % GROUPS) + (8 * j) // GROUPS, 8)
    x = x_ref[...] + gate_ref[...] * y
    if final:
        x = _rms(x) * gfin_ref[...]
    xo_ref[...] = x


def _ffn_call(l_arr, h, x, mod5, b_row, g_final, wup, wconv, wdn, seq_len, final):
    t = x.shape[0]
    n_tiles = t // TM
    per = TM // HALO
    row = lambda w: pl.BlockSpec((TM, w), lambda i, l: (i, 0))
    whole = lambda a: pl.BlockSpec((None,) + a.shape[1:], lambda i, l: (l[0],) + (0,) * (a.ndim - 1))
    return pl.pallas_call(
        functools.partial(_ffn_kernel, tiles_per_seq=seq_len // TM, final=final),
        grid_spec=pltpu.PrefetchScalarGridSpec(
            num_scalar_prefetch=1, grid=(n_tiles,),
            in_specs=[row(D_MODEL),
                      pl.BlockSpec((HALO, D_MODEL), lambda i, l: (jnp.maximum(i * per - 1, 0), 0)),
                      pl.BlockSpec((HALO, D_MODEL), lambda i, l: (jnp.minimum((i + 1) * per, n_tiles * per - 1), 0)),
                      row(D_MODEL),
                      pl.BlockSpec((None, None, None, 1, D_MODEL), lambda i, l: (l[0], b_row(i), 5, 0, 0)),
                      pl.BlockSpec((1, D_MODEL), lambda i, l: (0, 0)),
                      whole(wup), whole(wconv), whole(wdn)],
            out_specs=row(D_MODEL),
            scratch_shapes=[pltpu.VMEM((TM + 2 * HALO, D_MODEL), BF16), pltpu.VMEM((TM, D_FF), BF16),
                            pltpu.VMEM((D_MODEL // LANE, TM, LANE), F32)]),
        out_shape=jax.ShapeDtypeStruct((t, D_MODEL), F32),
        compiler_params=_cparams(("arbitrary",)),
        name="ffn_final" if final else "ffn",
    )(l_arr, h, h, h, x, mod5, g_final, wup, wconv, wdn)


def _dft_mats(n):
    k = jnp.arange(n, dtype=jnp.int32)
    ang = ((k[:, None] * k[None, :]) % n).astype(F32) * (2.0 * math.pi / n)
    s = 1.0 / math.sqrt(n)
    return (jnp.cos(ang) * s).astype(BF16), (jnp.sin(ang) * s).astype(BF16)


def _channel_dft():
    k = np.arange(F_GROUP_W)
    ang = 2.0 * np.pi * ((k[:, None] * k[None, :]) % F_GROUP_W) / F_GROUP_W
    eye = np.eye(F_GROUPS)
    s = 1.0 / math.sqrt(F_GROUP_W)
    return np.concatenate([np.kron(eye, np.cos(ang) * s), np.kron(eye, np.sin(ang) * s)], axis=1).astype(np.float32)


def kernel(x_prompt, x_sample, c, cache_mla_ckv, cache_mla_krope, cache_diff_k, cache_diff_v, c_ctx, w_ada, b_ada,
           g_mix_norm, g_ffn_norm, w_in, g_kv_norm, w_uk, w_uv, lam_q1, lam_k1, lam_q2, lam_k2, g_diff_subln,
           w_out, w_up, w_conv, w_down, g_final):
    batch_c, seq_c, _ = x_prompt.shape
    batch_l, seq_l, _ = x_sample.shape
    depth = w_in.shape[0]
    assert seq_c % TM == 0 and seq_l % TM == 0 and seq_l % GRID_W == 0 and D_FF % FF_CHUNK == 0
    assert seq_c % TQ_CTX == 0 and seq_l % TQ_LAT == 0

    w1 = _take(w_in.astype(BF16), _W1_COLS[True])
    w2 = _take(jnp.concatenate([w_uk, w_uv], axis=-1).astype(BF16), _W2_COLS)
    wo_a = _take(w_out.astype(BF16), _WOUT_ROWS, axis=1)
    wo_f = w_out[:, HEADS * (MLA_DV + DIFF_DV):, :].astype(BF16)
    wup = _take(w_up.astype(BF16), _FFN_COLS)
    wconv = _take(w_conv, _FFN_COLS)
    wdn = w_down.astype(BF16)
    vec = lambda a: a.reshape(depth, 1, a.shape[-1])
    g_mix, g_ffn, g_kv = vec(g_mix_norm), vec(g_ffn_norm), vec(g_kv_norm)
    gsub = jnp.concatenate([jnp.ones_like(g_diff_subln), g_diff_subln], axis=-1).reshape(depth, 1, LANE)
    lam_init = np.asarray([0.8 - 0.6 * math.exp(-0.3 * l) for l in range(depth)], np.float32)
    lam_init_rows = jnp.asarray(np.broadcast_to(lam_init[:, None, None], (depth, 1, LANE)))
    sub_rows = jnp.asarray(np.broadcast_to((1.0 - lam_init)[:, None, None], (depth, 1, LANE)).astype(np.float32))
    g_fin = g_final.reshape(1, D_MODEL)
    cs64 = jnp.asarray(_channel_dft()).astype(BF16)
    tables = tuple(jnp.asarray(a) for a in _rope_tables(seq_l))
    dft_c, dft_l = _dft_mats(seq_c), _dft_mats(seq_l)

    n_rows = -(-(batch_l + 1) // 8) * 8
    cond = jnp.zeros((n_rows, D_MODEL), F32).at[:batch_l].set(c).at[batch_l].set(c_ctx)
    mod, lam = _modulation_call(cond, w_ada, b_ada, lam_q1, lam_k1, lam_q2, lam_k2, lam_init_rows)
    mod5 = mod.reshape(depth, n_rows, w_ada.shape[2] // D_MODEL, 1, D_MODEL)
    tiles_l = seq_l // TM
    row_l = lambda i: i // tiles_l
    row_c = lambda i: batch_l

    flat = lambda a: a.reshape(a.shape[0], a.shape[1], a.shape[2], -1)
    past_kv = _cacheprep_call(cache_mla_ckv, cache_mla_krope, flat(cache_diff_k), flat(cache_diff_v), w2,
                              jnp.asarray(_E_KR).astype(BF16), jnp.asarray(_E_KD).astype(BF16),
                              jnp.asarray(_E_VD).astype(BF16))

    x_c = x_prompt.reshape(batch_c * seq_c, D_MODEL)
    x_l = x_sample.reshape(batch_l * seq_l, D_MODEL)
    caches = tuple(jnp.zeros((batch_c, depth, seq_c, w), F32)
                   for w in (MLA_RANK, MLA_ROPE, HEADS * 2 * DIFF_DH, HEADS * DIFF_DV))
    for l in range(depth):
        l_arr = jnp.full((1,), l, jnp.int32)
        final = l == depth - 1
        qm, km, qd, kd, vm, vd, ab, *caches = _inproj_call(
            l_arr, x_c, mod5, row_c, g_mix, w1, g_kv, w2, cs64, None, seq_c, caches)
        k3 = lambda a: a.reshape(batch_c, seq_c, HW)
        o = _attn_call(l_arr, lam, gsub, sub_rows, qm, qd, [(k3(km), k3(kd), k3(vm), k3(vd))],
                       batch_c, seq_c, TQ_CTX, HEADS)
        y = _dft_call(*dft_c, ab, batch_c, seq_c)
        x_c, h2 = _outproj_call(l_arr, o, y, x_c, mod5, row_c, g_ffn, wo_a, wo_f)
        x_c = _ffn_call(l_arr, h2, x_c, mod5, row_c, g_fin, wup, wconv, wdn, seq_c, final)
        qm, km, qd, kd, vm, vd, ab = _inproj_call(
            l_arr, x_l, mod5, row_l, g_mix, w1, g_kv, w2, cs64, tables, seq_l, None)
        k3 = lambda a: a.reshape(batch_l, seq_l, HW)
        o = _attn_call(l_arr, lam, gsub, sub_rows, qm, qd, [(k3(km), k3(kd), k3(vm), k3(vd)), past_kv],
                       batch_l, seq_l, TQ_LAT, HEADS_LAT)
        y = _dft_call(*dft_l, ab, batch_l, seq_l)
        x_l, h2 = _outproj_call(l_arr, o, y, x_l, mod5, row_l, g_ffn, wo_a, wo_f)
        x_l = _ffn_call(l_arr, h2, x_l, mod5, row_l, g_fin, wup, wconv, wdn, seq_l, final)

    new_ckv, new_kr, new_kd, new_vd = caches
    return (x_c.reshape(batch_c, seq_c, D_MODEL),
            x_l.reshape(batch_l, seq_l, D_MODEL),
            new_ckv, new_kr,
            new_kd.reshape(batch_c, depth, seq_c, HEADS, 2 * DIFF_DH),
            new_vd.reshape(batch_c, depth, seq_c, HEADS, DIFF_DV))
```

```python
import functools
import math

import numpy as np
import jax
import jax.numpy as jnp
from jax import lax
from jax.experimental import pallas as pl
from jax.experimental.pallas import tpu as pltpu

F32 = jnp.float32
BF16 = jnp.bfloat16

D_MODEL = 1024
GRID_W = 64
HEADS = 6
MLA_NOPE, MLA_ROPE, MLA_DV, MLA_RANK = 64, 32, 64, 128
DIFF_DH, DIFF_DV = 32, 64
F_GROUPS, F_GROUP_W = 4, 64
F_WIDTH = F_GROUPS * F_GROUP_W
D_FF = 2816
ROPE_BASE = 10000.0
EPS = 1e-6
MLA_SCALE = (MLA_NOPE + MLA_ROPE) ** -0.5
DIFF_SCALE = DIFF_DH ** -0.5
LOG2E = math.log2(math.e)

_QM0 = 0
_CKV0 = HEADS * (MLA_NOPE + MLA_ROPE)
_KR0 = _CKV0 + MLA_RANK
_QD0 = _KR0 + MLA_ROPE
_KD0 = _QD0 + HEADS * 2 * DIFF_DH
_VD0 = _KD0 + HEADS * 2 * DIFF_DH
_UF0 = _VD0 + HEADS * DIFF_DV

LANE = 128
HW = HEADS * LANE
TM = 256
TQ_CTX = 256
TQ_LAT = 1024
HEADS_LAT = 2
TQ_SUB = 512
BB_CTX = 4
PHASED_MAX_KEYS = 512
FF_CHUNK = 256
N_FF_CHUNKS = D_FF // FF_CHUNK
GROUPS = TM // 8
HALO = 8
VMEM_LIMIT = 56 * 1024 * 1024


def _cparams(sem):
    return pltpu.CompilerParams(dimension_semantics=sem, vmem_limit_bytes=VMEM_LIMIT)


def _w1_columns():
    z = lambda n: [-1] * n
    r = lambda a, n: list(range(a, a + n))
    cols = []
    cols += r(_CKV0, MLA_RANK)
    cols += z(MLA_NOPE) + r(_KR0, MLA_ROPE) + z(LANE - MLA_NOPE - MLA_ROPE)
    for h in range(HEADS):
        b = _QM0 + h * (MLA_NOPE + MLA_ROPE)
        cols += r(b, MLA_NOPE) + r(b + MLA_NOPE, MLA_ROPE) + r(b + MLA_NOPE, MLA_ROPE)
    cols += r(_QD0, _UF0 + F_WIDTH - _QD0)
    cols += r(_KR0, MLA_ROPE) + z(LANE - MLA_ROPE)
    return np.asarray(cols, np.int32)


_W1_COLS = _w1_columns()
DIFF_W = HEADS * 2 * DIFF_DH
assert DIFF_W == HEADS * DIFF_DV and 2 * DIFF_DH == LANE // 2
_C_CKV, _C_KR, _C_QM = 0, LANE, 2 * LANE
_C_QD = _C_QM + HW
_C_KD = _C_QD + DIFF_W
_C_VD = _C_KD + DIFF_W
_C_UF = _C_VD + DIFF_W
_C_XKR = _C_UF + F_WIDTH


def _w2_columns():
    cols = []
    for base in (0, HEADS * MLA_NOPE):
        for h in range(HEADS):
            cols += list(range(base + h * 64, base + (h + 1) * 64)) + [-1] * 64
    return np.asarray(cols, np.int32)


_W2_COLS = _w2_columns()


def _wout_rows():
    rows = []
    for h in range(HEADS):
        rows += list(range(h * MLA_DV, (h + 1) * MLA_DV))
        rows += list(range(HEADS * MLA_DV + h * DIFF_DV, HEADS * MLA_DV + (h + 1) * DIFF_DV))
    return np.asarray(rows, np.int32)


_WOUT_ROWS = _wout_rows()


def _take(w, idx, axis=-1):
    axis = axis % w.ndim
    pieces, i, n = [], 0, len(idx)
    while i < n:
        j = i + 1
        if idx[i] < 0:
            while j < n and idx[j] < 0:
                j += 1
            shape = w.shape[:axis] + (j - i,) + w.shape[axis + 1:]
            pieces.append(jnp.zeros(shape, w.dtype))
        else:
            while j < n and idx[j] == idx[j - 1] + 1:
                j += 1
            pieces.append(lax.slice_in_dim(w, int(idx[i]), int(idx[i]) + j - i, axis=axis))
        i = j
    return jnp.concatenate(pieces, axis=axis)


def _selection(src_of_col, n_src):
    e = np.zeros((n_src, len(src_of_col)), np.float32)
    for c, s in enumerate(src_of_col):
        if s >= 0:
            e[s, c] = 1.0
    return e


def _cache_selections():
    kr, up = [], []
    for h in range(HEADS):
        kr += [-1] * (MLA_NOPE + MLA_ROPE) + list(range(MLA_ROPE))
        up += [-1] * (LANE // 2) + list(range(h * LANE // 2, (h + 1) * LANE // 2))
    return _selection(kr, MLA_ROPE), _selection(up, DIFF_W)


_E_KR, _E_UP = _cache_selections()


def _rope_tables(n_lat):
    t = np.arange(n_lat)
    row = (t // GRID_W).astype(np.float64)
    col = (t % GRID_W).astype(np.float64)
    quarter = MLA_ROPE // 4
    inv = ROPE_BASE ** (-np.arange(quarter, dtype=np.float64) / quarter)
    inv = inv.astype(np.float32).astype(np.float64)
    ang = np.concatenate([row[:, None] * inv, row[:, None] * inv, col[:, None] * inv, col[:, None] * inv], axis=1)
    ang = ang.astype(np.float32)
    cos32 = np.cos(ang).astype(np.float32)
    sign = np.concatenate([-np.ones(quarter), np.ones(quarter)] * 2).astype(np.float32)
    sin32 = np.sin(ang).astype(np.float32) * sign
    one, zero = np.ones((n_lat, 32), np.float32), np.zeros((n_lat, 32), np.float32)
    cm = np.concatenate([one, one, cos32, one], axis=1)
    sm = np.concatenate([zero, zero, sin32, zero], axis=1)
    cd = np.concatenate([cos32] * 4, axis=1)
    sd = np.concatenate([sin32] * 4, axis=1)
    return cm, sm, cd, sd


def _mod_kernel(cond_ref, w_ref, b_ref, q1_ref, k1_ref, q2_ref, k2_ref, li_ref, mod_ref, lam_ref):
    a = cond_ref[...]
    a = a * jax.nn.sigmoid(a)
    acc = jnp.dot(a.astype(BF16), w_ref[...].astype(BF16), preferred_element_type=F32)
    mod_ref[...] = acc + b_ref[...]
    d1 = jnp.sum(q1_ref[...] * k1_ref[...], axis=-1, keepdims=True)
    d2 = jnp.sum(q2_ref[...] * k2_ref[...], axis=-1, keepdims=True)
    lam_ref[...] = jnp.exp(d1) - jnp.exp(d2) + li_ref[...]


def _modulation_call(cond, w_ada, b_ada, lam_q1, lam_k1, lam_q2, lam_k2, lam_init_rows):
    depth = w_ada.shape[0]
    n_sec = w_ada.shape[2] // D_MODEL
    rows = cond.shape[0]
    vec = lambda a: a.reshape(depth, 1, a.shape[-1])
    lam_spec = pl.BlockSpec((None, 1, DIFF_DH), lambda l, s: (l, 0, 0))
    return pl.pallas_call(
        _mod_kernel,
        grid=(depth, n_sec),
        in_specs=[
            pl.BlockSpec((rows, D_MODEL), lambda l, s: (0, 0)),
            pl.BlockSpec((None, D_MODEL, D_MODEL), lambda l, s: (l, 0, s)),
            pl.BlockSpec((None, 1, D_MODEL), lambda l, s: (l, 0, s)),
            lam_spec, lam_spec, lam_spec, lam_spec,
            pl.BlockSpec((None, 1, LANE), lambda l, s: (l, 0, 0)),
        ],
        out_specs=[
            pl.BlockSpec((None, rows, D_MODEL), lambda l, s: (l, 0, s)),
            pl.BlockSpec((None, 1, LANE), lambda l, s: (l, 0, 0)),
        ],
        out_shape=[
            jax.ShapeDtypeStruct((depth, rows, n_sec * D_MODEL), F32),
            jax.ShapeDtypeStruct((depth, 1, LANE), F32),
        ],
        compiler_params=_cparams(("arbitrary", "arbitrary")),
        name="modulation",
    )(cond, w_ada, vec(b_ada), vec(lam_q1), vec(lam_k1), vec(lam_q2), vec(lam_k2), lam_init_rows)


def _rms(x):
    return x * lax.rsqrt(jnp.mean(x * x, axis=-1, keepdims=True) + EPS)


def _swap_halves(x):
    lane = lax.broadcasted_iota(jnp.int32, x.shape, 1)
    first = (lane & 8) == 0
    return jnp.where(first, pltpu.roll(x, LANE - 8, axis=1), pltpu.roll(x, 8, axis=1))


def _rope(x, c, s):
    return x * c + _swap_halves(x) * s


def _store_lane_blocks(ref, x):
    for k in range(ref.shape[0]):
        ref[k] = x[:, k * LANE:(k + 1) * LANE]


def _strided_rows(ref, start_of_group, stride):
    return jnp.concatenate(
        [jnp.concatenate([ref[k, pl.ds(start_of_group(j), 8, stride=stride), :] for k in range(ref.shape[0])], axis=1)
         for j in range(GROUPS)], axis=0)


def _inproj_kernel(*refs, rope, ctx, tiles_per_seq):
    it = iter(refs)
    l_ref = next(it)
    x_ref, shift_ref, scale_ref, g_ref, w1_ref, gkv_ref, w2_ref, cs_ref = (next(it) for _ in range(8))
    if rope:
        cm_ref, sm_ref, cd_ref, sd_ref = (next(it) for _ in range(4))
    if ctx:
        for _ in range(4):
            next(it)
    qm_ref, km_ref, qd_ref, kd_ref, vm_ref, vd_ref, ab_ref = (next(it) for _ in range(7))
    if ctx:
        ckv_o, kr_o, kdc_o, vdc_o = (next(it) for _ in range(4))
    h_scr = next(it)
    del l_ref
    upper = lax.broadcasted_iota(jnp.int32, (TM, LANE), 1) >= LANE // 2
    ones_hi = upper.astype(F32)

    x = x_ref[...]
    h = _rms(x) * g_ref[...] * (1.0 + scale_ref[...]) + shift_ref[...]
    h_scr[...] = h.astype(BF16)

    if rope:
        r0 = pl.multiple_of((pl.program_id(0) % tiles_per_seq) * TM, TM)
        cm, sm = cm_ref[pl.ds(r0, TM), :], sm_ref[pl.ds(r0, TM), :]
        cd, sd = cd_ref[pl.ds(r0, TM), :], sd_ref[pl.ds(r0, TM), :]

    def proj(c0, width):
        return jnp.dot(h_scr[...], w1_ref[:, c0:c0 + width], preferred_element_type=F32)

    acc = proj(_C_CKV, 2 * LANE)
    ckvn = _rms(acc[:, :LANE]) * gkv_ref[...]
    kr = acc[:, LANE:]
    if ctx:
        ckv_o[...] = ckvn
    if rope:
        kr = _rope(kr, cm, sm)
    kv2 = jnp.dot(ckvn.astype(BF16), w2_ref[...], preferred_element_type=F32)
    for hd in range(HEADS):
        blk = slice(hd * LANE, (hd + 1) * LANE)
        km_ref[:, blk] = (kv2[:, blk] + kr).astype(BF16)
        vm_ref[:, blk] = (kv2[:, HW + hd * LANE:HW + (hd + 1) * LANE] + ones_hi).astype(BF16)

    for j in range(HW // (2 * LANE)):
        acc = proj(_C_QM + j * 2 * LANE, 2 * LANE)
        for k in range(2):
            blk = acc[:, k * LANE:(k + 1) * LANE]
            if rope:
                blk = _rope(blk, cm, sm)
            col = (2 * j + k) * LANE
            qm_ref[:, col:col + LANE] = (blk * (MLA_SCALE * LOG2E)).astype(BF16)

    qd_c, kd_c, vd_c = proj(_C_QD, DIFF_W), proj(_C_KD, DIFF_W), proj(_C_VD, DIFF_W)
    if ctx:
        kdc_o[...] = kd_c
        vdc_o[...] = vd_c
    swap = lambda a: pltpu.roll(a, LANE // 2, axis=1)
    for j in range(DIFF_W // LANE):
        pair = slice(j * LANE, (j + 1) * LANE)
        blk_a, blk_b = slice(2 * j * LANE, (2 * j + 1) * LANE), slice((2 * j + 1) * LANE, (2 * j + 2) * LANE)
        q = qd_c[:, pair] * (DIFF_SCALE * LOG2E)
        k = kd_c[:, pair]
        v = vd_c[:, pair]
        q_rot, k_rot = (_rope(q, cd, sd), _rope(k, cd, sd)) if rope else (q, k)
        qd_ref[:, blk_a] = jnp.where(upper, swap(q), q_rot).astype(BF16)
        qd_ref[:, blk_b] = jnp.where(upper, q, swap(q_rot)).astype(BF16)
        kd_ref[:, blk_a] = jnp.where(upper, 0.0, k_rot).astype(BF16)
        kd_ref[:, blk_b] = jnp.where(upper, 0.0, swap(k_rot)).astype(BF16)
        vd_ref[:, blk_a] = jnp.where(upper, swap(v), 1.0).astype(BF16)
        vd_ref[:, blk_b] = jnp.where(upper, v, 1.0).astype(BF16)

    uf = proj(_C_UF, F_WIDTH)
    ab_ref[...] = jnp.dot(uf.astype(BF16), cs_ref[...], preferred_element_type=F32).astype(BF16)

    if ctx:
        kr_o[...] = proj(_C_XKR, LANE)[:, :MLA_ROPE]


def _inproj_call(l_arr, x, mod5, b_row, g_mix, w1, g_kv, w2, cs64, tables, seq_len, caches):
    t = x.shape[0]
    n_tiles = t // TM
    tiles_per_seq = seq_len // TM
    rope = tables is not None
    ctx = caches is not None
    n1 = _C_XKR + LANE if ctx else _C_XKR
    row = lambda w: pl.BlockSpec((TM, w), lambda i, l: (i, 0))
    modspec = lambda sec: pl.BlockSpec((None, None, None, 1, D_MODEL), lambda i, l: (l[0], b_row(i), sec, 0, 0))
    in_specs = [
        row(D_MODEL), modspec(0), modspec(1),
        pl.BlockSpec((None, 1, D_MODEL), lambda i, l: (l[0], 0, 0)),
        pl.BlockSpec((None, D_MODEL, n1), lambda i, l: (l[0], 0, 0)),
        pl.BlockSpec((None, 1, MLA_RANK), lambda i, l: (l[0], 0, 0)),
        pl.BlockSpec((None, MLA_RANK, 2 * HW), lambda i, l: (l[0], 0, 0)),
        pl.BlockSpec((F_WIDTH, 2 * F_WIDTH), lambda i, l: (0, 0)),
    ]
    args = [x, mod5, mod5, g_mix, w1, g_kv, w2, cs64]
    if rope:
        in_specs += [pl.BlockSpec((seq_len, LANE), lambda i, l: (0, 0))] * 4
        args += list(tables)
    out_specs = [row(HW)] * 6 + [row(2 * F_WIDTH)]
    out_shape = [jax.ShapeDtypeStruct((t, HW), BF16)] * 6 + [jax.ShapeDtypeStruct((t, 2 * F_WIDTH), BF16)]
    aliases = {}
    if ctx:
        for k, a in enumerate(caches):
            aliases[1 + len(args)] = len(out_shape)
            in_specs.append(pl.BlockSpec(memory_space=pl.ANY))
            args.append(a)
            out_specs.append(pl.BlockSpec((None, None, TM, a.shape[-1]),
                                          lambda i, l: (i // tiles_per_seq, l[0], i % tiles_per_seq, 0)))
            out_shape.append(jax.ShapeDtypeStruct(a.shape, a.dtype))
    return pl.pallas_call(
        functools.partial(_inproj_kernel, rope=rope, ctx=ctx, tiles_per_seq=tiles_per_seq),
        grid_spec=pltpu.PrefetchScalarGridSpec(
            num_scalar_prefetch=1, grid=(n_tiles,), in_specs=in_specs, out_specs=out_specs,
            scratch_shapes=[pltpu.VMEM((TM, D_MODEL), BF16)]),
        out_shape=out_shape,
        input_output_aliases=aliases,
        compiler_params=_cparams(("arbitrary",)),
        name="inproj_ctx" if ctx else "inproj_lat",
    )(l_arr, *args)


def _cacheprep_kernel(ckv_ref, kr_ref, kd_ref, vd_ref, w2_ref, ekr_ref, eup_ref,
                      km_ref, kdo_ref, vm_ref, vdo_ref):
    dot = lambda a, b: jnp.dot(a.astype(BF16), b, preferred_element_type=F32)
    kv2 = dot(ckv_ref[...], w2_ref[...])
    ones_hi = ((lax.broadcasted_iota(jnp.int32, (1, HW), 1) % LANE) >= LANE // 2).astype(F32)
    km_ref[...] = (kv2[:, :HW] + dot(kr_ref[...], ekr_ref[...])).astype(BF16)
    kdo_ref[...] = dot(kd_ref[...], eup_ref[...]).astype(BF16)
    vm_ref[...] = (kv2[:, HW:] + ones_hi).astype(BF16)
    vdo_ref[...] = (dot(vd_ref[...], eup_ref[...]) + (1.0 - ones_hi)).astype(BF16)


def _cacheprep_call(cache_ckv, cache_kr, cache_kd, cache_vd, w2, ekr, eup):
    b, depth, p = cache_ckv.shape[:3]
    cspec = lambda w: pl.BlockSpec((None, None, p, w), lambda l, i: (i, l, 0, 0))
    const = lambda a: pl.BlockSpec(a.shape, lambda l, i: (0,) * a.ndim)
    ospec = pl.BlockSpec((None, None, p, HW), lambda l, i: (l, i, 0, 0))
    oshape = jax.ShapeDtypeStruct((depth, b, p, HW), BF16)
    return pl.pallas_call(
        _cacheprep_kernel,
        grid=(depth, b),
        in_specs=[cspec(MLA_RANK), cspec(MLA_ROPE), cspec(HEADS * 2 * DIFF_DH), cspec(HEADS * DIFF_DV),
                  pl.BlockSpec((None, MLA_RANK, 2 * HW), lambda l, i: (l, 0, 0)),
                  const(ekr), const(eup)],
        out_specs=[ospec] * 4,
        out_shape=[oshape] * 4,
        compiler_params=_cparams(("arbitrary", "arbitrary")),
        name="cacheprep",
    )(cache_ckv, cache_kr, cache_kd, cache_vd, w2, ekr, eup)


def _softmax_pv(q, k_refs, v_refs, blk):
    nt = (((1,), (1,)), ((), ()))
    s = [lax.dot_general(q, k[:, blk], nt, preferred_element_type=F32) for k in k_refs]
    m = functools.reduce(jnp.maximum, [jnp.max(si, axis=-1, keepdims=True) for si in s])
    o = functools.reduce(jnp.add, [jnp.dot(jnp.exp2(si - m).astype(BF16), v[:, blk], preferred_element_type=F32)
                                   for si, v in zip(s, v_refs)])
    return o / pltpu.roll(o, LANE // 2, axis=1)


def _attn_kernel(*refs, n_seg, heads, tq, phased):
    it = iter(refs)
    l_ref = next(it)
    lam_ref, gsub_ref, sub_ref, qm_ref, qd_ref = (next(it) for _ in range(5))
    km = [next(it) for _ in range(n_seg)]
    kd = [next(it) for _ in range(n_seg)]
    vm = [next(it) for _ in range(n_seg)]
    vd = [next(it) for _ in range(n_seg)]
    o_ref = next(it)
    del l_ref

    sub = min(tq, TQ_SUB)
    lane = lax.broadcasted_iota(jnp.int32, (sub, LANE), 1)
    upper = lane >= LANE // 2
    first = (lane // DIFF_DH) % 2 == 0
    nt = (((1,), (1,)), ((), ()))
    problems = [(bi, slice(hd * LANE, (hd + 1) * LANE), slice(r * sub, (r + 1) * sub))
                for bi in range(qm_ref.shape[0]) for hd in range(heads) for r in range(tq // sub)]
    at = lambda refs, bi: [r.at[bi] for r in refs]

    if phased:
        qs, ks, vs, lanes = [], [], [], []
        for bi, blk, rows in problems:
            qd = qd_ref[bi, rows, blk]
            zero = jnp.zeros_like(qd)
            qs += [qm_ref[bi, rows, blk], jnp.where(first, qd, zero), jnp.where(first, zero, qd)]
            ks += [at(km, bi), at(kd, bi), at(kd, bi)]
            vs += [at(vm, bi), at(vd, bi), at(vd, bi)]
            lanes += [blk] * 3
        scores = [[lax.dot_general(q, k[:, blk], nt, preferred_element_type=F32) for k in kk]
                  for q, kk, blk in zip(qs, ks, lanes)]
        maxes = [functools.reduce(jnp.maximum, [jnp.max(si, axis=-1, keepdims=True) for si in s]) for s in scores]
        probs = [[jnp.exp2(si - m).astype(BF16) for si in s] for s, m in zip(scores, maxes)]
        outs = [functools.reduce(jnp.add, [jnp.dot(pi, v[:, blk], preferred_element_type=F32)
                                           for pi, v in zip(p, vv)])
                for p, vv, blk in zip(probs, vs, lanes)]
        outs = [o / pltpu.roll(o, LANE // 2, axis=1) for o in outs]
    else:
        outs = []
        for bi, blk, rows in problems:
            qd = qd_ref[bi, rows, blk]
            zero = jnp.zeros_like(qd)
            outs.append(_softmax_pv(qm_ref[bi, rows, blk], at(km, bi), at(vm, bi), blk))
            outs.append(_softmax_pv(jnp.where(first, qd, zero), at(kd, bi), at(vd, bi), blk))
            outs.append(_softmax_pv(jnp.where(first, zero, qd), at(kd, bi), at(vd, bi), blk))

    for n, (bi, blk, rows) in enumerate(problems):
        o_m, o_1, o_2 = outs[3 * n:3 * n + 3]
        o_d = jnp.where(upper, o_1 - lam_ref[...] * o_2, 0.0)
        ms = jnp.sum(o_d * o_d, axis=-1, keepdims=True) * (1.0 / DIFF_DV)
        o_d = o_d * lax.rsqrt(ms + EPS) * gsub_ref[...] * sub_ref[...]
        o_ref[bi, rows, blk] = jnp.where(upper, o_d, o_m).astype(BF16)


def _attn_call(l_arr, lam, gsub, sub, qm, qd, segs, batch, n_q, tq, heads, bb):
    n_seg = len(segs)
    width = heads * LANE
    q3 = lambda a: a.reshape(batch, n_q, HW)
    layer_vec = pl.BlockSpec((None, 1, LANE), lambda b, h, i, l: (l[0], 0, 0))
    qspec = pl.BlockSpec((bb, tq, width), lambda b, h, i, l: (b, i, h))
    kspecs, kargs, n_keys = [], [], 0
    for which in range(4):
        for seg in segs:
            a = seg[which]
            if a.ndim == 4:
                kspecs.append(pl.BlockSpec((None, bb, a.shape[2], width), lambda b, h, i, l: (l[0], b, 0, h)))
            else:
                kspecs.append(pl.BlockSpec((bb, a.shape[1], width), lambda b, h, i, l: (b, 0, h)))
            kargs.append(a)
            n_keys += a.shape[-2] if which == 0 else 0
    return pl.pallas_call(
        functools.partial(_attn_kernel, n_seg=n_seg, heads=heads, tq=tq, phased=n_keys <= PHASED_MAX_KEYS),
        grid_spec=pltpu.PrefetchScalarGridSpec(
            num_scalar_prefetch=1, grid=(batch // bb, HEADS // heads, n_q // tq),
            in_specs=[layer_vec, layer_vec, layer_vec, qspec, qspec] + kspecs,
            out_specs=qspec),
        out_shape=jax.ShapeDtypeStruct((batch, n_q, HW), BF16),
        compiler_params=_cparams(("arbitrary", "arbitrary", "arbitrary")),
        name="attention_%dseg" % n_seg,
    )(l_arr, lam, gsub, sub, q3(qm), q3(qd), *kargs).reshape(batch * n_q, HW)


def _dft_kernel(c_ref, s_ref, ab_ref, y_ref):
    ya = jnp.dot(c_ref[...], ab_ref[:, :F_WIDTH], preferred_element_type=F32)
    yb = jnp.dot(s_ref[...], ab_ref[:, F_WIDTH:], preferred_element_type=F32)
    y_ref[...] = (ya - yb).astype(BF16)


def _dft_call(cn, sn, ab, batch, n):
    tr = min(n, TM)
    mat = pl.BlockSpec((tr, n), lambda b, r: (r, 0))
    return pl.pallas_call(
        _dft_kernel,
        grid=(batch, n // tr),
        in_specs=[mat, mat, pl.BlockSpec((None, n, 2 * F_WIDTH), lambda b, r: (b, 0, 0))],
        out_specs=pl.BlockSpec((None, tr, F_WIDTH), lambda b, r: (b, r, 0)),
        out_shape=jax.ShapeDtypeStruct((batch, n, F_WIDTH), BF16),
        compiler_params=_cparams(("arbitrary", "arbitrary")),
        name="position_dft",
    )(cn, sn, ab.reshape(batch, n, 2 * F_WIDTH)).reshape(batch * n, F_WIDTH)


def _outproj_kernel(l_ref, o_ref, y_ref, x_ref, gate_ref, shift_ref, scale_ref, g_ref, wa_ref, wf_ref,
                    xo_ref, h_ref, hs_ref):
    del l_ref
    o = jnp.dot(o_ref[...], wa_ref[...], preferred_element_type=F32)
    o = o + jnp.dot(y_ref[...], wf_ref[...], preferred_element_type=F32)
    x = x_ref[...] + gate_ref[...] * o
    xo_ref[...] = x
    h = _rms(x) * g_ref[...] * (1.0 + scale_ref[...]) + shift_ref[...]
    for j in range(GROUPS):
        start = 8 * ((8 * j) % GROUPS) + (8 * j) // GROUPS
        for k in range(D_MODEL // LANE):
            hs_ref[k, pl.ds(start, 8, stride=8), :] = h[8 * j:8 * j + 8, k * LANE:(k + 1) * LANE]
    h_ref[...] = jnp.concatenate([hs_ref[k] for k in range(D_MODEL // LANE)], axis=1).astype(BF16)


def _outproj_call(l_arr, o, y, x, mod5, b_row, g_ffn, wo_a, wo_f):
    t = x.shape[0]
    row = lambda w: pl.BlockSpec((TM, w), lambda i, l: (i, 0))
    modspec = lambda sec: pl.BlockSpec((None, None, None, 1, D_MODEL), lambda i, l: (l[0], b_row(i), sec, 0, 0))
    return pl.pallas_call(
        _outproj_kernel,
        grid_spec=pltpu.PrefetchScalarGridSpec(
            num_scalar_prefetch=1, grid=(t // TM,),
            in_specs=[row(HW), row(F_WIDTH), row(D_MODEL), modspec(2), modspec(3), modspec(4),
                      pl.BlockSpec((None, 1, D_MODEL), lambda i, l: (l[0], 0, 0)),
                      pl.BlockSpec((None, HW, D_MODEL), lambda i, l: (l[0], 0, 0)),
                      pl.BlockSpec((None, F_WIDTH, D_MODEL), lambda i, l: (l[0], 0, 0))],
            out_specs=[row(D_MODEL), row(D_MODEL)],
            scratch_shapes=[pltpu.VMEM((D_MODEL // LANE, TM, LANE), F32)]),
        out_shape=[jax.ShapeDtypeStruct((t, D_MODEL), F32), jax.ShapeDtypeStruct((t, D_MODEL), BF16)],
        compiler_params=_cparams(("arbitrary",)),
        name="outproj",
    )(l_arr, o, y, x, mod5, mod5, mod5, g_ffn, wo_a, wo_f)


def _ffn_kernel(l_ref, h_ref, hp_ref, hn_ref, x_ref, gate_ref, gfin_ref, wup_ref, wconv_ref, wdn_ref,
                xo_ref, hx_ref, act_ref, ys_ref, *, tiles_per_seq, final):
    del l_ref
    i = pl.program_id(0)
    has_prev = (i % tiles_per_seq) != 0
    has_next = (i % tiles_per_seq) != tiles_per_seq - 1
    hx_ref[0:TM, :] = h_ref[...]
    hx_ref[TM:TM + HALO, :] = jnp.where(has_prev, hp_ref[...], jnp.zeros_like(hp_ref))
    hx_ref[TM + HALO:, :] = jnp.where(has_next, hn_ref[...], jnp.zeros_like(hn_ref))

    for c in range(N_FF_CHUNKS):
        halves = []
        for cols in (slice(c * FF_CHUNK, (c + 1) * FF_CHUNK),
                     slice(D_FF + c * FF_CHUNK, D_FF + (c + 1) * FF_CHUNK)):
            u = jnp.dot(hx_ref[...], wup_ref[:, cols], preferred_element_type=F32)
            wc = wconv_ref[:, cols]
            mid = u[:TM]
            wrap_dn = jnp.concatenate([u[TM + HALO - 1:TM + HALO], mid[TM - 8:TM - 1]], axis=0)
            wrap_up = jnp.concatenate([mid[1:8], u[TM + HALO:TM + HALO + 1]], axis=0)
            dn = jnp.concatenate([wrap_dn, mid[:TM - 8]], axis=0)
            up = jnp.concatenate([mid[8:], wrap_up], axis=0)
            halves.append(dn * wc[0:1, :] + mid * wc[1:2, :] + up * wc[2:3, :])
        gate, val = halves
        act_ref[:, c * FF_CHUNK:(c + 1) * FF_CHUNK] = (gate * jax.nn.sigmoid(gate) * val).astype(BF16)

    _store_lane_blocks(ys_ref, jnp.dot(act_ref[...], wdn_ref[...], preferred_element_type=F32))
    y = _strided_rows(ys_ref, lambda j: 8 * ((8 * j) % GROUPS) + (8 * j) // GROUPS, 8)
    x = x_ref[...] + gate_ref[...] * y
    if final:
        x = _rms(x) * gfin_ref[...]
    xo_ref[...] = x


def _ffn_call(l_arr, h, x, mod5, b_row, g_final, wup, wconv, wdn, seq_len, final):
    t = x.shape[0]
    n_tiles = t // TM
    per = TM // HALO
    row = lambda w: pl.BlockSpec((TM, w), lambda i, l: (i, 0))
    whole = lambda a: pl.BlockSpec((None,) + a.shape[1:], lambda i, l: (l[0],) + (0,) * (a.ndim - 1))
    return pl.pallas_call(
        functools.partial(_ffn_kernel, tiles_per_seq=seq_len // TM, final=final),
        grid_spec=pltpu.PrefetchScalarGridSpec(
            num_scalar_prefetch=1, grid=(n_tiles,),
            in_specs=[row(D_MODEL),
                      pl.BlockSpec((HALO, D_MODEL), lambda i, l: (jnp.maximum(i * per - 1, 0), 0)),
                      pl.BlockSpec((HALO, D_MODEL), lambda i, l: (jnp.minimum((i + 1) * per, n_tiles * per - 1), 0)),
                      row(D_MODEL),
                      pl.BlockSpec((None, None, None, 1, D_MODEL), lambda i, l: (l[0], b_row(i), 5, 0, 0)),
                      pl.BlockSpec((1, D_MODEL), lambda i, l: (0, 0)),
                      whole(wup), whole(wconv), whole(wdn)],
            out_specs=row(D_MODEL),
            scratch_shapes=[pltpu.VMEM((TM + 2 * HALO, D_MODEL), BF16), pltpu.VMEM((TM, D_FF), BF16),
                            pltpu.VMEM((D_MODEL // LANE, TM, LANE), F32)]),
        out_shape=jax.ShapeDtypeStruct((t, D_MODEL), F32),
        compiler_params=_cparams(("arbitrary",)),
        name="ffn_final" if final else "ffn",
    )(l_arr, h, h, h, x, mod5, g_final, wup, wconv, wdn)


def _dft_mats(n, inner=64):
    j = jnp.arange(n, dtype=jnp.int32)[:, None]
    ang = lambda k: ((j * k[None, :]) % n).astype(F32) * (2.0 * math.pi / n)
    a = ang(jnp.arange(n // inner, dtype=jnp.int32) * inner)[:, :, None]
    b = ang(jnp.arange(inner, dtype=jnp.int32))[:, None, :]
    s = 1.0 / math.sqrt(n)
    ca, sa, cb, sb = jnp.cos(a) * s, jnp.sin(a) * s, jnp.cos(b), jnp.sin(b)
    return ((ca * cb - sa * sb).reshape(n, n).astype(BF16), (sa * cb + ca * sb).reshape(n, n).astype(BF16))


def _channel_dft():
    k = np.arange(F_GROUP_W)
    ang = 2.0 * np.pi * ((k[:, None] * k[None, :]) % F_GROUP_W) / F_GROUP_W
    eye = np.eye(F_GROUPS)
    s = 1.0 / math.sqrt(F_GROUP_W)
    return np.concatenate([np.kron(eye, np.cos(ang) * s), np.kron(eye, np.sin(ang) * s)], axis=1).astype(np.float32)


def kernel(x_prompt, x_sample, c, cache_mla_ckv, cache_mla_krope, cache_diff_k, cache_diff_v, c_ctx, w_ada, b_ada,
           g_mix_norm, g_ffn_norm, w_in, g_kv_norm, w_uk, w_uv, lam_q1, lam_k1, lam_q2, lam_k2, g_diff_subln,
           w_out, w_up, w_conv, w_down, g_final):
    batch_c, seq_c, _ = x_prompt.shape
    batch_l, seq_l, _ = x_sample.shape
    depth = w_in.shape[0]
    assert seq_c % TM == 0 and seq_l % TM == 0 and seq_l % GRID_W == 0 and D_FF % FF_CHUNK == 0
    assert seq_c % TQ_CTX == 0 and seq_l % TQ_LAT == 0 and batch_c % BB_CTX == 0

    w1 = _take(w_in.astype(BF16), _W1_COLS)
    w2 = _take(jnp.concatenate([w_uk, w_uv], axis=-1).astype(BF16), _W2_COLS)
    wo_a = _take(w_out.astype(BF16), _WOUT_ROWS, axis=1)
    wo_f = w_out[:, HEADS * (MLA_DV + DIFF_DV):, :].astype(BF16)
    wup, wconv, wdn = w_up.astype(BF16), w_conv, w_down.astype(BF16)
    vec = lambda a: a.reshape(depth, 1, a.shape[-1])
    g_mix, g_ffn, g_kv = vec(g_mix_norm), vec(g_ffn_norm), vec(g_kv_norm)
    gsub = jnp.concatenate([jnp.ones_like(g_diff_subln), g_diff_subln], axis=-1).reshape(depth, 1, LANE)
    lam_init = np.asarray([0.8 - 0.6 * math.exp(-0.3 * l) for l in range(depth)], np.float32)
    lam_init_rows = jnp.asarray(np.broadcast_to(lam_init[:, None, None], (depth, 1, LANE)))
    sub_rows = jnp.asarray(np.broadcast_to((1.0 - lam_init)[:, None, None], (depth, 1, LANE)).astype(np.float32))
    g_fin = g_final.reshape(1, D_MODEL)
    cs64 = jnp.asarray(_channel_dft()).astype(BF16)
    tables = tuple(jnp.asarray(a) for a in _rope_tables(seq_l))
    dft_c, dft_l = _dft_mats(seq_c), _dft_mats(seq_l)

    n_rows = -(-(batch_l + 1) // 8) * 8
    cond = jnp.zeros((n_rows, D_MODEL), F32).at[:batch_l].set(c).at[batch_l].set(c_ctx)
    mod, lam = _modulation_call(cond, w_ada, b_ada, lam_q1, lam_k1, lam_q2, lam_k2, lam_init_rows)
    mod5 = mod.reshape(depth, n_rows, w_ada.shape[2] // D_MODEL, 1, D_MODEL)
    tiles_l = seq_l // TM
    row_l = lambda i: i // tiles_l
    row_c = lambda i: batch_l

    flat = lambda a: a.reshape(a.shape[0], a.shape[1], a.shape[2], -1)
    past_kv = _cacheprep_call(cache_mla_ckv, cache_mla_krope, flat(cache_diff_k), flat(cache_diff_v), w2,
                              jnp.asarray(_E_KR).astype(BF16), jnp.asarray(_E_UP).astype(BF16))

    x_c = x_prompt.reshape(batch_c * seq_c, D_MODEL)
    x_l = x_sample.reshape(batch_l * seq_l, D_MODEL)
    caches = tuple(jnp.zeros((batch_c, depth, seq_c, w), F32)
                   for w in (MLA_RANK, MLA_ROPE, HEADS * 2 * DIFF_DH, HEADS * DIFF_DV))
    for l in range(depth):
        l_arr = jnp.full((1,), l, jnp.int32)
        final = l == depth - 1
        qm, km, qd, kd, vm, vd, ab, *caches = _inproj_call(
            l_arr, x_c, mod5, row_c, g_mix, w1, g_kv, w2, cs64, None, seq_c, caches)
        k3 = lambda a: a.reshape(batch_c, seq_c, HW)
        o = _attn_call(l_arr, lam, gsub, sub_rows, qm, qd, [(k3(km), k3(kd), k3(vm), k3(vd))],
                       batch_c, seq_c, TQ_CTX, HEADS, BB_CTX)
        y = _dft_call(*dft_c, ab, batch_c, seq_c)
        x_c, h2 = _outproj_call(l_arr, o, y, x_c, mod5, row_c, g_ffn, wo_a, wo_f)
        x_c = _ffn_call(l_arr, h2, x_c, mod5, row_c, g_fin, wup, wconv, wdn, seq_c, final)
        qm, km, qd, kd, vm, vd, ab = _inproj_call(
            l_arr, x_l, mod5, row_l, g_mix, w1, g_kv, w2, cs64, tables, seq_l, None)
        k3 = lambda a: a.reshape(batch_l, seq_l, HW)
        o = _attn_call(l_arr, lam, gsub, sub_rows, qm, qd, [(k3(km), k3(kd), k3(vm), k3(vd)), past_kv],
                       batch_l, seq_l, TQ_LAT, HEADS_LAT, 1)
        y = _dft_call(*dft_l, ab, batch_l, seq_l)
        x_l, h2 = _outproj_call(l_arr, o, y, x_l, mod5, row_l, g_ffn, wo_a, wo_f)
        x_l = _ffn_call(l_arr, h2, x_l, mod5, row_l, g_fin, wup, wconv, wdn, seq_l, final)

    new_ckv, new_kr, new_kd, new_vd = caches
    return (x_c.reshape(batch_c, seq_c, D_MODEL),
            x_l.reshape(batch_l, seq_l, D_MODEL),
            new_ckv, new_kr,
            new_kd.reshape(batch_c, depth, seq_c, HEADS, 2 * DIFF_DH),
            new_vd.reshape(batch_c, depth, seq_c, HEADS, DIFF_DV))
```

```python
import functools
import math

import numpy as np
import jax
import jax.numpy as jnp
from jax import lax
from jax.experimental import pallas as pl
from jax.experimental.pallas import tpu as pltpu

F32 = jnp.float32
BF16 = jnp.bfloat16

D_MODEL = 1024
GRID_W = 64
HEADS = 6
MLA_NOPE, MLA_ROPE, MLA_DV, MLA_RANK = 64, 32, 64, 128
DIFF_DH, DIFF_DV = 32, 64
F_GROUPS, F_GROUP_W = 4, 64
F_WIDTH = F_GROUPS * F_GROUP_W
D_FF = 2816
ROPE_BASE = 10000.0
EPS = 1e-6
MLA_SCALE = (MLA_NOPE + MLA_ROPE) ** -0.5
DIFF_SCALE = DIFF_DH ** -0.5
LOG2E = math.log2(math.e)

_QM0 = 0
_CKV0 = HEADS * (MLA_NOPE + MLA_ROPE)
_KR0 = _CKV0 + MLA_RANK
_QD0 = _KR0 + MLA_ROPE
_KD0 = _QD0 + HEADS * 2 * DIFF_DH
_VD0 = _KD0 + HEADS * 2 * DIFF_DH
_UF0 = _VD0 + HEADS * DIFF_DV

LANE = 128
HW = HEADS * LANE
TM = 256
TM_IN_LAT = 512
TQ_CTX = 256
TQ_LAT = 1024
HEADS_LAT = 2
TQ_SUB = 512
BB_CTX = 4
DFT_ROWS_PER_STEP = 8192
PHASED_MAX_KEYS = 512
FF_CHUNK = 256
N_FF_CHUNKS = D_FF // FF_CHUNK
GROUPS = TM // 8
HALO = 8
VMEM_LIMIT = 56 * 1024 * 1024


def _cparams(sem):
    return pltpu.CompilerParams(dimension_semantics=sem, vmem_limit_bytes=VMEM_LIMIT)


def _w1_columns():
    z = lambda n: [-1] * n
    r = lambda a, n: list(range(a, a + n))
    cols = []
    cols += r(_CKV0, MLA_RANK)
    cols += z(MLA_NOPE) + r(_KR0, MLA_ROPE) + z(LANE - MLA_NOPE - MLA_ROPE)
    for h in range(HEADS):
        b = _QM0 + h * (MLA_NOPE + MLA_ROPE)
        cols += r(b, MLA_NOPE) + r(b + MLA_NOPE, MLA_ROPE) + r(b + MLA_NOPE, MLA_ROPE)
    cols += r(_QD0, _UF0 + F_WIDTH - _QD0)
    cols += r(_KR0, MLA_ROPE) + z(LANE - MLA_ROPE)
    return np.asarray(cols, np.int32)


_W1_COLS = _w1_columns()
DIFF_W = HEADS * 2 * DIFF_DH
assert DIFF_W == HEADS * DIFF_DV and 2 * DIFF_DH == LANE // 2
_C_CKV, _C_KR, _C_QM = 0, LANE, 2 * LANE
_C_QD = _C_QM + HW
_C_KD = _C_QD + DIFF_W
_C_VD = _C_KD + DIFF_W
_C_UF = _C_VD + DIFF_W
_C_XKR = _C_UF + F_WIDTH


def _w2_columns():
    cols = []
    for base in (0, HEADS * MLA_NOPE):
        for h in range(HEADS):
            cols += list(range(base + h * 64, base + (h + 1) * 64)) + [-1] * 64
    return np.asarray(cols, np.int32)


_W2_COLS = _w2_columns()


def _wout_rows():
    rows = []
    for h in range(HEADS):
        rows += list(range(h * MLA_DV, (h + 1) * MLA_DV))
        rows += list(range(HEADS * MLA_DV + h * DIFF_DV, HEADS * MLA_DV + (h + 1) * DIFF_DV))
    return np.asarray(rows, np.int32)


_WOUT_ROWS = _wout_rows()


def _take(w, idx, axis=-1):
    axis = axis % w.ndim
    pieces, i, n = [], 0, len(idx)
    while i < n:
        j = i + 1
        if idx[i] < 0:
            while j < n and idx[j] < 0:
                j += 1
            shape = w.shape[:axis] + (j - i,) + w.shape[axis + 1:]
            pieces.append(jnp.zeros(shape, w.dtype))
        else:
            while j < n and idx[j] == idx[j - 1] + 1:
                j += 1
            pieces.append(lax.slice_in_dim(w, int(idx[i]), int(idx[i]) + j - i, axis=axis))
        i = j
    return jnp.concatenate(pieces, axis=axis)


def _selection(src_of_col, n_src):
    e = np.zeros((n_src, len(src_of_col)), np.float32)
    for c, s in enumerate(src_of_col):
        if s >= 0:
            e[s, c] = 1.0
    return e


def _cache_selections():
    kr, up = [], []
    for h in range(HEADS):
        kr += [-1] * (MLA_NOPE + MLA_ROPE) + list(range(MLA_ROPE))
        up += [-1] * (LANE // 2) + list(range(h * LANE // 2, (h + 1) * LANE // 2))
    return _selection(kr, MLA_ROPE), _selection(up, DIFF_W)


_E_KR, _E_UP = _cache_selections()


def _rope_tables(n_lat):
    t = np.arange(n_lat)
    row = (t // GRID_W).astype(np.float64)
    col = (t % GRID_W).astype(np.float64)
    quarter = MLA_ROPE // 4
    inv = ROPE_BASE ** (-np.arange(quarter, dtype=np.float64) / quarter)
    inv = inv.astype(np.float32).astype(np.float64)
    ang = np.concatenate([row[:, None] * inv, row[:, None] * inv, col[:, None] * inv, col[:, None] * inv], axis=1)
    ang = ang.astype(np.float32)
    cos32 = np.cos(ang).astype(np.float32)
    sign = np.concatenate([-np.ones(quarter), np.ones(quarter)] * 2).astype(np.float32)
    sin32 = np.sin(ang).astype(np.float32) * sign
    one, zero = np.ones((n_lat, 32), np.float32), np.zeros((n_lat, 32), np.float32)
    cm = np.concatenate([one, one, cos32, one], axis=1)
    sm = np.concatenate([zero, zero, sin32, zero], axis=1)
    cd = np.concatenate([cos32] * 4, axis=1)
    sd = np.concatenate([sin32] * 4, axis=1)
    return cm, sm, cd, sd


def _mod_kernel(cond_ref, w_ref, b_ref, q1_ref, k1_ref, q2_ref, k2_ref, li_ref, mod_ref, lam_ref):
    a = cond_ref[...]
    a = a * jax.nn.sigmoid(a)
    acc = jnp.dot(a.astype(BF16), w_ref[...].astype(BF16), preferred_element_type=F32)
    mod_ref[...] = acc + b_ref[...]
    d1 = jnp.sum(q1_ref[...] * k1_ref[...], axis=-1, keepdims=True)
    d2 = jnp.sum(q2_ref[...] * k2_ref[...], axis=-1, keepdims=True)
    lam_ref[...] = jnp.exp(d1) - jnp.exp(d2) + li_ref[...]


def _modulation_call(cond, w_ada, b_ada, lam_q1, lam_k1, lam_q2, lam_k2, lam_init_rows):
    depth = w_ada.shape[0]
    n_sec = w_ada.shape[2] // D_MODEL
    rows = cond.shape[0]
    vec = lambda a: a.reshape(depth, 1, a.shape[-1])
    lam_spec = pl.BlockSpec((None, 1, DIFF_DH), lambda l, s: (l, 0, 0))
    return pl.pallas_call(
        _mod_kernel,
        grid=(depth, n_sec),
        in_specs=[
            pl.BlockSpec((rows, D_MODEL), lambda l, s: (0, 0)),
            pl.BlockSpec((None, D_MODEL, D_MODEL), lambda l, s: (l, 0, s)),
            pl.BlockSpec((None, 1, D_MODEL), lambda l, s: (l, 0, s)),
            lam_spec, lam_spec, lam_spec, lam_spec,
            pl.BlockSpec((None, 1, LANE), lambda l, s: (l, 0, 0)),
        ],
        out_specs=[
            pl.BlockSpec((None, rows, D_MODEL), lambda l, s: (l, 0, s)),
            pl.BlockSpec((None, 1, LANE), lambda l, s: (l, 0, 0)),
        ],
        out_shape=[
            jax.ShapeDtypeStruct((depth, rows, n_sec * D_MODEL), F32),
            jax.ShapeDtypeStruct((depth, 1, LANE), F32),
        ],
        compiler_params=_cparams(("arbitrary", "arbitrary")),
        name="modulation",
    )(cond, w_ada, vec(b_ada), vec(lam_q1), vec(lam_k1), vec(lam_q2), vec(lam_k2), lam_init_rows)


def _rms(x):
    return x * lax.rsqrt(jnp.mean(x * x, axis=-1, keepdims=True) + EPS)


def _swap_halves(x):
    lane = lax.broadcasted_iota(jnp.int32, x.shape, 1)
    first = (lane & 8) == 0
    return jnp.where(first, pltpu.roll(x, LANE - 8, axis=1), pltpu.roll(x, 8, axis=1))


def _rope(x, c, s):
    return x * c + _swap_halves(x) * s


def _store_lane_blocks(ref, x):
    for k in range(ref.shape[0]):
        ref[k] = x[:, k * LANE:(k + 1) * LANE]


def _strided_rows(ref, start_of_group, stride):
    return jnp.concatenate(
        [jnp.concatenate([ref[k, pl.ds(start_of_group(j), 8, stride=stride), :] for k in range(ref.shape[0])], axis=1)
         for j in range(GROUPS)], axis=0)


def _inproj_kernel(*refs, rope, ctx, tiles_per_seq):
    it = iter(refs)
    l_ref = next(it)
    x_ref, shift_ref, scale_ref, g_ref, w1_ref, gkv_ref, w2_ref, cs_ref = (next(it) for _ in range(8))
    if rope:
        cm_ref, sm_ref, cd_ref, sd_ref = (next(it) for _ in range(4))
    if ctx:
        for _ in range(4):
            next(it)
    qm_ref, km_ref, qd_ref, kd_ref, vm_ref, vd_ref, ab_ref = (next(it) for _ in range(7))
    if ctx:
        ckv_o, kr_o, kdc_o, vdc_o = (next(it) for _ in range(4))
    h_scr = next(it)
    del l_ref
    tm = x_ref.shape[0]
    upper = lax.broadcasted_iota(jnp.int32, (tm, LANE), 1) >= LANE // 2
    ones_hi = upper.astype(F32)

    x = x_ref[...]
    h = _rms(x) * g_ref[...] * (1.0 + scale_ref[...]) + shift_ref[...]
    h_scr[...] = h.astype(BF16)

    if rope:
        r0 = pl.multiple_of((pl.program_id(0) % tiles_per_seq) * tm, tm)
        cm, sm = cm_ref[pl.ds(r0, tm), :], sm_ref[pl.ds(r0, tm), :]
        cd, sd = cd_ref[pl.ds(r0, tm), :], sd_ref[pl.ds(r0, tm), :]

    def proj(c0, width):
        return jnp.dot(h_scr[...], w1_ref[:, c0:c0 + width], preferred_element_type=F32)

    acc = proj(_C_CKV, 2 * LANE)
    ckvn = _rms(acc[:, :LANE]) * gkv_ref[...]
    kr = acc[:, LANE:]
    if ctx:
        ckv_o[...] = ckvn
    if rope:
        kr = _rope(kr, cm, sm)
    kv2 = jnp.dot(ckvn.astype(BF16), w2_ref[...], preferred_element_type=F32)
    for hd in range(HEADS):
        blk = slice(hd * LANE, (hd + 1) * LANE)
        km_ref[:, blk] = (kv2[:, blk] + kr).astype(BF16)
        vm_ref[:, blk] = (kv2[:, HW + hd * LANE:HW + (hd + 1) * LANE] + ones_hi).astype(BF16)

    for j in range(HW // (2 * LANE)):
        acc = proj(_C_QM + j * 2 * LANE, 2 * LANE)
        for k in range(2):
            blk = acc[:, k * LANE:(k + 1) * LANE]
            if rope:
                blk = _rope(blk, cm, sm)
            col = (2 * j + k) * LANE
            qm_ref[:, col:col + LANE] = (blk * (MLA_SCALE * LOG2E)).astype(BF16)

    qd_c, kd_c, vd_c = proj(_C_QD, DIFF_W), proj(_C_KD, DIFF_W), proj(_C_VD, DIFF_W)
    if ctx:
        kdc_o[...] = kd_c
        vdc_o[...] = vd_c
    swap = lambda a: pltpu.roll(a, LANE // 2, axis=1)
    for j in range(DIFF_W // LANE):
        pair = slice(j * LANE, (j + 1) * LANE)
        blk_a, blk_b = slice(2 * j * LANE, (2 * j + 1) * LANE), slice((2 * j + 1) * LANE, (2 * j + 2) * LANE)
        q = qd_c[:, pair] * (DIFF_SCALE * LOG2E)
        k = kd_c[:, pair]
        v = vd_c[:, pair]
        q_rot, k_rot = (_rope(q, cd, sd), _rope(k, cd, sd)) if rope else (q, k)
        qd_ref[:, blk_a] = jnp.where(upper, swap(q), q_rot).astype(BF16)
        qd_ref[:, blk_b] = jnp.where(upper, q, swap(q_rot)).astype(BF16)
        kd_ref[:, blk_a] = jnp.where(upper, 0.0, k_rot).astype(BF16)
        kd_ref[:, blk_b] = jnp.where(upper, 0.0, swap(k_rot)).astype(BF16)
        vd_ref[:, blk_a] = jnp.where(upper, swap(v), 1.0).astype(BF16)
        vd_ref[:, blk_b] = jnp.where(upper, v, 1.0).astype(BF16)

    uf = proj(_C_UF, F_WIDTH)
    ab_ref[...] = jnp.dot(uf.astype(BF16), cs_ref[...], preferred_element_type=F32).astype(BF16)

    if ctx:
        kr_o[...] = proj(_C_XKR, LANE)[:, :MLA_ROPE]


def _inproj_call(l_arr, x, mod5, b_row, g_mix, w1, g_kv, w2, cs64, tables, seq_len, caches, tm):
    t = x.shape[0]
    n_tiles = t // tm
    tiles_per_seq = seq_len // tm
    rope = tables is not None
    ctx = caches is not None
    n1 = _C_XKR + LANE if ctx else _C_XKR
    row = lambda w: pl.BlockSpec((tm, w), lambda i, l: (i, 0))
    modspec = lambda sec: pl.BlockSpec((None, None, None, 1, D_MODEL),
                                       lambda i, l: (l[0], b_row(i, tiles_per_seq), sec, 0, 0))
    in_specs = [
        row(D_MODEL), modspec(0), modspec(1),
        pl.BlockSpec((None, 1, D_MODEL), lambda i, l: (l[0], 0, 0)),
        pl.BlockSpec((None, D_MODEL, n1), lambda i, l: (l[0], 0, 0)),
        pl.BlockSpec((None, 1, MLA_RANK), lambda i, l: (l[0], 0, 0)),
        pl.BlockSpec((None, MLA_RANK, 2 * HW), lambda i, l: (l[0], 0, 0)),
        pl.BlockSpec((F_WIDTH, 2 * F_WIDTH), lambda i, l: (0, 0)),
    ]
    args = [x, mod5, mod5, g_mix, w1, g_kv, w2, cs64]
    if rope:
        in_specs += [pl.BlockSpec((seq_len, LANE), lambda i, l: (0, 0))] * 4
        args += list(tables)
    out_specs = [row(HW)] * 6 + [row(2 * F_WIDTH)]
    out_shape = [jax.ShapeDtypeStruct((t, HW), BF16)] * 6 + [jax.ShapeDtypeStruct((t, 2 * F_WIDTH), BF16)]
    aliases = {}
    if ctx:
        for k, a in enumerate(caches):
            aliases[1 + len(args)] = len(out_shape)
            in_specs.append(pl.BlockSpec(memory_space=pl.ANY))
            args.append(a)
            out_specs.append(pl.BlockSpec((None, None, tm, a.shape[-1]),
                                          lambda i, l: (i // tiles_per_seq, l[0], i % tiles_per_seq, 0)))
            out_shape.append(jax.ShapeDtypeStruct(a.shape, a.dtype))
    return pl.pallas_call(
        functools.partial(_inproj_kernel, rope=rope, ctx=ctx, tiles_per_seq=tiles_per_seq),
        grid_spec=pltpu.PrefetchScalarGridSpec(
            num_scalar_prefetch=1, grid=(n_tiles,), in_specs=in_specs, out_specs=out_specs,
            scratch_shapes=[pltpu.VMEM((tm, D_MODEL), BF16)]),
        out_shape=out_shape,
        input_output_aliases=aliases,
        compiler_params=_cparams(("arbitrary",)),
        name="inproj_ctx" if ctx else "inproj_lat",
    )(l_arr, *args)


def _cacheprep_kernel(ckv_ref, kr_ref, kd_ref, vd_ref, w2_ref, ekr_ref, eup_ref,
                      km_ref, kdo_ref, vm_ref, vdo_ref):
    dot = lambda a, b: jnp.dot(a.astype(BF16), b, preferred_element_type=F32)
    kv2 = dot(ckv_ref[...], w2_ref[...])
    ones_hi = ((lax.broadcasted_iota(jnp.int32, (1, HW), 1) % LANE) >= LANE // 2).astype(F32)
    km_ref[...] = (kv2[:, :HW] + dot(kr_ref[...], ekr_ref[...])).astype(BF16)
    kdo_ref[...] = dot(kd_ref[...], eup_ref[...]).astype(BF16)
    vm_ref[...] = (kv2[:, HW:] + ones_hi).astype(BF16)
    vdo_ref[...] = (dot(vd_ref[...], eup_ref[...]) + (1.0 - ones_hi)).astype(BF16)


def _cacheprep_call(cache_ckv, cache_kr, cache_kd, cache_vd, w2, ekr, eup):
    b, depth, p = cache_ckv.shape[:3]
    cspec = lambda w: pl.BlockSpec((None, None, p, w), lambda l, i: (i, l, 0, 0))
    const = lambda a: pl.BlockSpec(a.shape, lambda l, i: (0,) * a.ndim)
    ospec = pl.BlockSpec((None, None, p, HW), lambda l, i: (l, i, 0, 0))
    oshape = jax.ShapeDtypeStruct((depth, b, p, HW), BF16)
    return pl.pallas_call(
        _cacheprep_kernel,
        grid=(depth, b),
        in_specs=[cspec(MLA_RANK), cspec(MLA_ROPE), cspec(HEADS * 2 * DIFF_DH), cspec(HEADS * DIFF_DV),
                  pl.BlockSpec((None, MLA_RANK, 2 * HW), lambda l, i: (l, 0, 0)),
                  const(ekr), const(eup)],
        out_specs=[ospec] * 4,
        out_shape=[oshape] * 4,
        compiler_params=_cparams(("arbitrary", "arbitrary")),
        name="cacheprep",
    )(cache_ckv, cache_kr, cache_kd, cache_vd, w2, ekr, eup)


def _softmax_pv(q, k_refs, v_refs, blk):
    nt = (((1,), (1,)), ((), ()))
    s = [lax.dot_general(q, k[:, blk], nt, preferred_element_type=F32) for k in k_refs]
    m = functools.reduce(jnp.maximum, [jnp.max(si, axis=-1, keepdims=True) for si in s])
    o = functools.reduce(jnp.add, [jnp.dot(jnp.exp2(si - m).astype(BF16), v[:, blk], preferred_element_type=F32)
                                   for si, v in zip(s, v_refs)])
    return o / pltpu.roll(o, LANE // 2, axis=1)


def _attn_kernel(*refs, n_seg, heads, tq, phased):
    it = iter(refs)
    l_ref = next(it)
    lam_ref, gsub_ref, sub_ref, qm_ref, qd_ref = (next(it) for _ in range(5))
    km = [next(it) for _ in range(n_seg)]
    kd = [next(it) for _ in range(n_seg)]
    vm = [next(it) for _ in range(n_seg)]
    vd = [next(it) for _ in range(n_seg)]
    o_ref = next(it)
    del l_ref

    sub = min(tq, TQ_SUB)
    lane = lax.broadcasted_iota(jnp.int32, (sub, LANE), 1)
    upper = lane >= LANE // 2
    first = (lane // DIFF_DH) % 2 == 0
    nt = (((1,), (1,)), ((), ()))
    problems = [(bi, slice(hd * LANE, (hd + 1) * LANE), slice(r * sub, (r + 1) * sub))
                for bi in range(qm_ref.shape[0]) for hd in range(heads) for r in range(tq // sub)]
    at = lambda refs, bi: [r.at[bi] for r in refs]

    if phased:
        qs, ks, vs, lanes = [], [], [], []
        for bi, blk, rows in problems:
            qd = qd_ref[bi, rows, blk]
            zero = jnp.zeros_like(qd)
            qs += [qm_ref[bi, rows, blk], jnp.where(first, qd, zero), jnp.where(first, zero, qd)]
            ks += [at(km, bi), at(kd, bi), at(kd, bi)]
            vs += [at(vm, bi), at(vd, bi), at(vd, bi)]
            lanes += [blk] * 3
        scores = [[lax.dot_general(q, k[:, blk], nt, preferred_element_type=F32) for k in kk]
                  for q, kk, blk in zip(qs, ks, lanes)]
        maxes = [functools.reduce(jnp.maximum, [jnp.max(si, axis=-1, keepdims=True) for si in s]) for s in scores]
        probs = [[jnp.exp2(si - m).astype(BF16) for si in s] for s, m in zip(scores, maxes)]
        outs = [functools.reduce(jnp.add, [jnp.dot(pi, v[:, blk], preferred_element_type=F32)
                                           for pi, v in zip(p, vv)])
                for p, vv, blk in zip(probs, vs, lanes)]
        outs = [o / pltpu.roll(o, LANE // 2, axis=1) for o in outs]
    else:
        outs = []
        for bi, blk, rows in problems:
            qd = qd_ref[bi, rows, blk]
            zero = jnp.zeros_like(qd)
            outs.append(_softmax_pv(qm_ref[bi, rows, blk], at(km, bi), at(vm, bi), blk))
            outs.append(_softmax_pv(jnp.where(first, qd, zero), at(kd, bi), at(vd, bi), blk))
            outs.append(_softmax_pv(jnp.where(first, zero, qd), at(kd, bi), at(vd, bi), blk))

    for n, (bi, blk, rows) in enumerate(problems):
        o_m, o_1, o_2 = outs[3 * n:3 * n + 3]
        o_d = jnp.where(upper, o_1 - lam_ref[...] * o_2, 0.0)
        ms = jnp.sum(o_d * o_d, axis=-1, keepdims=True) * (1.0 / DIFF_DV)
        o_d = o_d * lax.rsqrt(ms + EPS) * gsub_ref[...] * sub_ref[...]
        o_ref[bi, rows, blk] = jnp.where(upper, o_d, o_m).astype(BF16)


def _attn_call(l_arr, lam, gsub, sub, qm, qd, segs, batch, n_q, tq, heads, bb):
    n_seg = len(segs)
    width = heads * LANE
    q3 = lambda a: a.reshape(batch, n_q, HW)
    layer_vec = pl.BlockSpec((None, 1, LANE), lambda b, h, i, l: (l[0], 0, 0))
    qspec = pl.BlockSpec((bb, tq, width), lambda b, h, i, l: (b, i, h))
    kspecs, kargs, n_keys = [], [], 0
    for which in range(4):
        for seg in segs:
            a = seg[which]
            if a.ndim == 4:
                kspecs.append(pl.BlockSpec((None, bb, a.shape[2], width), lambda b, h, i, l: (l[0], b, 0, h)))
            else:
                kspecs.append(pl.BlockSpec((bb, a.shape[1], width), lambda b, h, i, l: (b, 0, h)))
            kargs.append(a)
            n_keys += a.shape[-2] if which == 0 else 0
    return pl.pallas_call(
        functools.partial(_attn_kernel, n_seg=n_seg, heads=heads, tq=tq, phased=n_keys <= PHASED_MAX_KEYS),
        grid_spec=pltpu.PrefetchScalarGridSpec(
            num_scalar_prefetch=1, grid=(batch // bb, HEADS // heads, n_q // tq),
            in_specs=[layer_vec, layer_vec, layer_vec, qspec, qspec] + kspecs,
            out_specs=qspec),
        out_shape=jax.ShapeDtypeStruct((batch, n_q, HW), BF16),
        compiler_params=_cparams(("arbitrary", "arbitrary", "arbitrary")),
        name="attention_%dseg" % n_seg,
    )(l_arr, lam, gsub, sub, q3(qm), q3(qd), *kargs).reshape(batch * n_q, HW)


def _dft_kernel(c_ref, s_ref, ab_ref, y_ref):
    for bi in range(ab_ref.shape[0]):
        ya = jnp.dot(c_ref[...], ab_ref[bi, :, :F_WIDTH], preferred_element_type=F32)
        yb = jnp.dot(s_ref[...], ab_ref[bi, :, F_WIDTH:], preferred_element_type=F32)
        y_ref[bi] = (ya - yb).astype(BF16)


def _dft_call(cn, sn, ab, batch, n):
    tr = min(n, TM)
    bb = math.gcd(batch, max(1, DFT_ROWS_PER_STEP // n))
    mat = pl.BlockSpec((tr, n), lambda b, r: (r, 0))
    return pl.pallas_call(
        _dft_kernel,
        grid=(batch // bb, n // tr),
        in_specs=[mat, mat, pl.BlockSpec((bb, n, 2 * F_WIDTH), lambda b, r: (b, 0, 0))],
        out_specs=pl.BlockSpec((bb, tr, F_WIDTH), lambda b, r: (b, r, 0)),
        out_shape=jax.ShapeDtypeStruct((batch, n, F_WIDTH), BF16),
        compiler_params=_cparams(("arbitrary", "arbitrary")),
        name="position_dft",
    )(cn, sn, ab.reshape(batch, n, 2 * F_WIDTH)).reshape(batch * n, F_WIDTH)


def _mixffn_kernel(l_ref, o_ref, op_ref, on_ref, y_ref, yp_ref, yn_ref, x_ref, xp_ref, xn_ref,
                   ga_ref, sf_ref, cf_ref, gf_ref, g_ref, gfin_ref, wa_ref, wf_ref, wup_ref, wconv_ref, wdn_ref,
                   xo_ref, lo_ref, ly_ref, hs_ref, hx_ref, act_ref, ys_ref, *, tiles_per_seq, final):
    del l_ref
    i = pl.program_id(0)
    has_prev = (i % tiles_per_seq) != 0
    has_next = (i % tiles_per_seq) != tiles_per_seq - 1
    for dst, parts in ((lo_ref, (o_ref, op_ref, on_ref)), (ly_ref, (y_ref, yp_ref, yn_ref))):
        dst[0:TM, :] = parts[0][...]
        dst[TM:TM + HALO, :] = parts[1][...]
        dst[TM + HALO:, :] = parts[2][...]
    attn = jnp.dot(lo_ref[...], wa_ref[...], preferred_element_type=F32)
    attn = attn + jnp.dot(ly_ref[...], wf_ref[...], preferred_element_type=F32)
    x = jnp.concatenate([x_ref[...], xp_ref[...], xn_ref[...]], axis=0) + ga_ref[...] * attn
    xo_ref[...] = x[:TM]
    h = _rms(x) * g_ref[...] * (1.0 + cf_ref[...]) + sf_ref[...]
    for j in range(GROUPS):
        start = 8 * ((8 * j) % GROUPS) + (8 * j) // GROUPS
        for k in range(D_MODEL // LANE):
            hs_ref[k, pl.ds(start, 8, stride=8), :] = h[8 * j:8 * j + 8, k * LANE:(k + 1) * LANE]
    hx_ref[0:TM, :] = jnp.concatenate([hs_ref[k] for k in range(D_MODEL // LANE)], axis=1).astype(BF16)
    zero = jnp.zeros((HALO, D_MODEL), F32)
    hx_ref[TM:TM + HALO, :] = jnp.where(has_prev, h[TM:TM + HALO], zero).astype(BF16)
    hx_ref[TM + HALO:, :] = jnp.where(has_next, h[TM + HALO:], zero).astype(BF16)

    for c in range(N_FF_CHUNKS):
        halves = []
        for cols in (slice(c * FF_CHUNK, (c + 1) * FF_CHUNK),
                     slice(D_FF + c * FF_CHUNK, D_FF + (c + 1) * FF_CHUNK)):
            u = jnp.dot(hx_ref[...], wup_ref[:, cols], preferred_element_type=F32)
            wc = wconv_ref[:, cols]
            mid = u[:TM]
            wrap_dn = jnp.concatenate([u[TM + HALO - 1:TM + HALO], mid[TM - 8:TM - 1]], axis=0)
            wrap_up = jnp.concatenate([mid[1:8], u[TM + HALO:TM + HALO + 1]], axis=0)
            dn = jnp.concatenate([wrap_dn, mid[:TM - 8]], axis=0)
            up = jnp.concatenate([mid[8:], wrap_up], axis=0)
            halves.append(dn * wc[0:1, :] + mid * wc[1:2, :] + up * wc[2:3, :])
        gate, val = halves
        act_ref[:, c * FF_CHUNK:(c + 1) * FF_CHUNK] = (gate * jax.nn.sigmoid(gate) * val).astype(BF16)

    _store_lane_blocks(ys_ref, jnp.dot(act_ref[...], wdn_ref[...], preferred_element_type=F32))
    y = _strided_rows(ys_ref, lambda j: 8 * ((8 * j) % GROUPS) + (8 * j) // GROUPS, 8)
    x = xo_ref[...] + gf_ref[...] * y
    if final:
        x = _rms(x) * gfin_ref[...]
    xo_ref[...] = x


def _mixffn_call(l_arr, o, y, x, mod5, b_row, g_ffn, g_final, wo_a, wo_f, wup, wconv, wdn, seq_len, final):
    t = x.shape[0]
    n_tiles = t // TM
    per = TM // HALO
    ext = TM + 2 * HALO

    def with_halos(w):
        return [pl.BlockSpec((TM, w), lambda i, l: (i, 0)),
                pl.BlockSpec((HALO, w), lambda i, l: (jnp.maximum(i * per - 1, 0), 0)),
                pl.BlockSpec((HALO, w), lambda i, l: (jnp.minimum((i + 1) * per, n_tiles * per - 1), 0))]

    tiles_per_seq = seq_len // TM
    modspec = lambda sec: pl.BlockSpec((None, None, None, 1, D_MODEL),
                                       lambda i, l: (l[0], b_row(i, tiles_per_seq), sec, 0, 0))
    whole = lambda a: pl.BlockSpec((None,) + a.shape[1:], lambda i, l: (l[0],) + (0,) * (a.ndim - 1))
    return pl.pallas_call(
        functools.partial(_mixffn_kernel, tiles_per_seq=tiles_per_seq, final=final),
        grid_spec=pltpu.PrefetchScalarGridSpec(
            num_scalar_prefetch=1, grid=(n_tiles,),
            in_specs=with_halos(HW) + with_halos(F_WIDTH) + with_halos(D_MODEL)
            + [modspec(2), modspec(3), modspec(4), modspec(5), whole(g_ffn),
               pl.BlockSpec((1, D_MODEL), lambda i, l: (0, 0)),
               whole(wo_a), whole(wo_f), whole(wup), whole(wconv), whole(wdn)],
            out_specs=pl.BlockSpec((TM, D_MODEL), lambda i, l: (i, 0)),
            scratch_shapes=[pltpu.VMEM((ext, HW), BF16), pltpu.VMEM((ext, F_WIDTH), BF16),
                            pltpu.VMEM((D_MODEL // LANE, TM, LANE), F32), pltpu.VMEM((ext, D_MODEL), BF16),
                            pltpu.VMEM((TM, D_FF), BF16), pltpu.VMEM((D_MODEL // LANE, TM, LANE), F32)]),
        out_shape=jax.ShapeDtypeStruct((t, D_MODEL), F32),
        compiler_params=_cparams(("arbitrary",)),
        name="mix_ffn_final" if final else "mix_ffn",
    )(l_arr, o, o, o, y, y, y, x, x, x, mod5, mod5, mod5, mod5, g_ffn, g_final, wo_a, wo_f, wup, wconv, wdn)


def _dft_mats(n, inner=64):
    j = jnp.arange(n, dtype=jnp.int32)[:, None]
    ang = lambda k: ((j * k[None, :]) % n).astype(F32) * (2.0 * math.pi / n)
    a = ang(jnp.arange(n // inner, dtype=jnp.int32) * inner)[:, :, None]
    b = ang(jnp.arange(inner, dtype=jnp.int32))[:, None, :]
    s = 1.0 / math.sqrt(n)
    ca, sa, cb, sb = jnp.cos(a) * s, jnp.sin(a) * s, jnp.cos(b), jnp.sin(b)
    return ((ca * cb - sa * sb).reshape(n, n).astype(BF16), (sa * cb + ca * sb).reshape(n, n).astype(BF16))


def _channel_dft():
    k = np.arange(F_GROUP_W)
    ang = 2.0 * np.pi * ((k[:, None] * k[None, :]) % F_GROUP_W) / F_GROUP_W
    eye = np.eye(F_GROUPS)
    s = 1.0 / math.sqrt(F_GROUP_W)
    return np.concatenate([np.kron(eye, np.cos(ang) * s), np.kron(eye, np.sin(ang) * s)], axis=1).astype(np.float32)


def kernel(x_prompt, x_sample, c, cache_mla_ckv, cache_mla_krope, cache_diff_k, cache_diff_v, c_ctx, w_ada, b_ada,
           g_mix_norm, g_ffn_norm, w_in, g_kv_norm, w_uk, w_uv, lam_q1, lam_k1, lam_q2, lam_k2, g_diff_subln,
           w_out, w_up, w_conv, w_down, g_final):
    batch_c, seq_c, _ = x_prompt.shape
    batch_l, seq_l, _ = x_sample.shape
    depth = w_in.shape[0]
    assert seq_c % TM == 0 and seq_l % TM == 0 and seq_l % GRID_W == 0 and D_FF % FF_CHUNK == 0
    assert seq_c % TQ_CTX == 0 and seq_l % TQ_LAT == 0 and batch_c % BB_CTX == 0 and seq_l % TM_IN_LAT == 0

    w1 = _take(w_in.astype(BF16), _W1_COLS)
    w2 = _take(jnp.concatenate([w_uk, w_uv], axis=-1).astype(BF16), _W2_COLS)
    wo_a = _take(w_out.astype(BF16), _WOUT_ROWS, axis=1)
    wo_f = w_out[:, HEADS * (MLA_DV + DIFF_DV):, :].astype(BF16)
    wup, wconv, wdn = w_up.astype(BF16), w_conv, w_down.astype(BF16)
    vec = lambda a: a.reshape(depth, 1, a.shape[-1])
    g_mix, g_ffn, g_kv = vec(g_mix_norm), vec(g_ffn_norm), vec(g_kv_norm)
    gsub = jnp.concatenate([jnp.ones_like(g_diff_subln), g_diff_subln], axis=-1).reshape(depth, 1, LANE)
    lam_init = np.asarray([0.8 - 0.6 * math.exp(-0.3 * l) for l in range(depth)], np.float32)
    lam_init_rows = jnp.asarray(np.broadcast_to(lam_init[:, None, None], (depth, 1, LANE)))
    sub_rows = jnp.asarray(np.broadcast_to((1.0 - lam_init)[:, None, None], (depth, 1, LANE)).astype(np.float32))
    g_fin = g_final.reshape(1, D_MODEL)
    cs64 = jnp.asarray(_channel_dft()).astype(BF16)
    tables = tuple(jnp.asarray(a) for a in _rope_tables(seq_l))
    dft_c, dft_l = _dft_mats(seq_c), _dft_mats(seq_l)

    n_rows = -(-(batch_l + 1) // 8) * 8
    cond = jnp.zeros((n_rows, D_MODEL), F32).at[:batch_l].set(c).at[batch_l].set(c_ctx)
    mod, lam = _modulation_call(cond, w_ada, b_ada, lam_q1, lam_k1, lam_q2, lam_k2, lam_init_rows)
    mod5 = mod.reshape(depth, n_rows, w_ada.shape[2] // D_MODEL, 1, D_MODEL)
    row_l = lambda i, tiles_per_seq: i // tiles_per_seq
    row_c = lambda i, tiles_per_seq: batch_l

    flat = lambda a: a.reshape(a.shape[0], a.shape[1], a.shape[2], -1)
    past_kv = _cacheprep_call(cache_mla_ckv, cache_mla_krope, flat(cache_diff_k), flat(cache_diff_v), w2,
                              jnp.asarray(_E_KR).astype(BF16), jnp.asarray(_E_UP).astype(BF16))

    x_c = x_prompt.reshape(batch_c * seq_c, D_MODEL)
    x_l = x_sample.reshape(batch_l * seq_l, D_MODEL)
    caches = tuple(jnp.zeros((batch_c, depth, seq_c, w), F32)
                   for w in (MLA_RANK, MLA_ROPE, HEADS * 2 * DIFF_DH, HEADS * DIFF_DV))
    for l in range(depth):
        l_arr = jnp.full((1,), l, jnp.int32)
        final = l == depth - 1
        qm, km, qd, kd, vm, vd, ab, *caches = _inproj_call(
            l_arr, x_c, mod5, row_c, g_mix, w1, g_kv, w2, cs64, None, seq_c, caches, TM)
        k3 = lambda a: a.reshape(batch_c, seq_c, HW)
        o = _attn_call(l_arr, lam, gsub, sub_rows, qm, qd, [(k3(km), k3(kd), k3(vm), k3(vd))],
                       batch_c, seq_c, TQ_CTX, HEADS, BB_CTX)
        y = _dft_call(*dft_c, ab, batch_c, seq_c)
        x_c = _mixffn_call(l_arr, o, y, x_c, mod5, row_c, g_ffn, g_fin, wo_a, wo_f, wup, wconv, wdn, seq_c, final)
        qm, km, qd, kd, vm, vd, ab = _inproj_call(
            l_arr, x_l, mod5, row_l, g_mix, w1, g_kv, w2, cs64, tables, seq_l, None, TM_IN_LAT)
        k3 = lambda a: a.reshape(batch_l, seq_l, HW)
        o = _attn_call(l_arr, lam, gsub, sub_rows, qm, qd, [(k3(km), k3(kd), k3(vm), k3(vd)), past_kv],
                       batch_l, seq_l, TQ_LAT, HEADS_LAT, 1)
        y = _dft_call(*dft_l, ab, batch_l, seq_l)
        x_l = _mixffn_call(l_arr, o, y, x_l, mod5, row_l, g_ffn, g_fin, wo_a, wo_f, wup, wconv, wdn, seq_l, final)

    new_ckv, new_kr, new_kd, new_vd = caches
    return (x_c.reshape(batch_c, seq_c, D_MODEL),
            x_l.reshape(batch_l, seq_l, D_MODEL),
            new_ckv, new_kr,
            new_kd.reshape(batch_c, depth, seq_c, HEADS, 2 * DIFF_DH),
            new_vd.reshape(batch_c, depth, seq_c, HEADS, DIFF_DV))
```

```python
import functools
import math

import numpy as np
import jax
import jax.numpy as jnp
from jax import lax
from jax.experimental import pallas as pl
from jax.experimental.pallas import tpu as pltpu

F32 = jnp.float32
BF16 = jnp.bfloat16

D_MODEL = 1024
GRID_W = 64
HEADS = 6
MLA_NOPE, MLA_ROPE, MLA_DV, MLA_RANK = 64, 32, 64, 128
DIFF_DH, DIFF_DV = 32, 64
F_GROUPS, F_GROUP_W = 4, 64
F_WIDTH = F_GROUPS * F_GROUP_W
D_FF = 2816
ROPE_BASE = 10000.0
EPS = 1e-6
MLA_SCALE = (MLA_NOPE + MLA_ROPE) ** -0.5
DIFF_SCALE = DIFF_DH ** -0.5
LOG2E = math.log2(math.e)

_QM0 = 0
_CKV0 = HEADS * (MLA_NOPE + MLA_ROPE)
_KR0 = _CKV0 + MLA_RANK
_QD0 = _KR0 + MLA_ROPE
_KD0 = _QD0 + HEADS * 2 * DIFF_DH
_VD0 = _KD0 + HEADS * 2 * DIFF_DH
_UF0 = _VD0 + HEADS * DIFF_DV

LANE = 128
HW = HEADS * LANE
TM = 256
TM_LAT = 512
TM_IN_LAT = 512
TM_IN_CTX = 512
TQ_CTX = 256
TQ_LAT = 1024
HEADS_LAT = 3
TQ_SUB = 512
BB_CTX = 4
DFT_ROWS_PER_STEP = 8192
PHASED_MAX_KEYS = 512
FF_CHUNK = 256
N_FF_CHUNKS = D_FF // FF_CHUNK
HALO = 8
VMEM_LIMIT = 56 * 1024 * 1024


def _cparams(sem):
    return pltpu.CompilerParams(dimension_semantics=sem, vmem_limit_bytes=VMEM_LIMIT)


def _w1_columns():
    z = lambda n: [-1] * n
    r = lambda a, n: list(range(a, a + n))
    cols = []
    cols += r(_CKV0, MLA_RANK)
    cols += z(MLA_NOPE) + r(_KR0, MLA_ROPE) + z(LANE - MLA_NOPE - MLA_ROPE)
    for h in range(HEADS):
        b = _QM0 + h * (MLA_NOPE + MLA_ROPE)
        cols += r(b, MLA_NOPE) + r(b + MLA_NOPE, MLA_ROPE) + r(b + MLA_NOPE, MLA_ROPE)
    cols += r(_QD0, _UF0 + F_WIDTH - _QD0)
    cols += r(_KR0, MLA_ROPE) + z(LANE - MLA_ROPE)
    return np.asarray(cols, np.int32)


_W1_COLS = _w1_columns()
DIFF_W = HEADS * 2 * DIFF_DH
assert DIFF_W == HEADS * DIFF_DV and 2 * DIFF_DH == LANE // 2
_C_CKV, _C_KR, _C_QM = 0, LANE, 2 * LANE
_C_QD = _C_QM + HW
_C_KD = _C_QD + DIFF_W
_C_VD = _C_KD + DIFF_W
_C_UF = _C_VD + DIFF_W
_C_XKR = _C_UF + F_WIDTH


def _w2_columns():
    cols = []
    for base in (0, HEADS * MLA_NOPE):
        for h in range(HEADS):
            cols += list(range(base + h * 64, base + (h + 1) * 64)) + [-1] * 64
    return np.asarray(cols, np.int32)


_W2_COLS = _w2_columns()


def _wout_rows():
    rows = []
    for h in range(HEADS):
        rows += list(range(h * MLA_DV, (h + 1) * MLA_DV))
        rows += list(range(HEADS * MLA_DV + h * DIFF_DV, HEADS * MLA_DV + (h + 1) * DIFF_DV))
    return np.asarray(rows, np.int32)


_WOUT_ROWS = _wout_rows()


def _take(w, idx, axis=-1):
    axis = axis % w.ndim
    pieces, i, n = [], 0, len(idx)
    while i < n:
        j = i + 1
        if idx[i] < 0:
            while j < n and idx[j] < 0:
                j += 1
            shape = w.shape[:axis] + (j - i,) + w.shape[axis + 1:]
            pieces.append(jnp.zeros(shape, w.dtype))
        else:
            while j < n and idx[j] == idx[j - 1] + 1:
                j += 1
            pieces.append(lax.slice_in_dim(w, int(idx[i]), int(idx[i]) + j - i, axis=axis))
        i = j
    return jnp.concatenate(pieces, axis=axis)


def _selection(src_of_col, n_src):
    e = np.zeros((n_src, len(src_of_col)), np.float32)
    for c, s in enumerate(src_of_col):
        if s >= 0:
            e[s, c] = 1.0
    return e


def _cache_selections():
    kr, up = [], []
    for h in range(HEADS):
        kr += [-1] * (MLA_NOPE + MLA_ROPE) + list(range(MLA_ROPE))
        up += [-1] * (LANE // 2) + list(range(h * LANE // 2, (h + 1) * LANE // 2))
    return _selection(kr, MLA_ROPE), _selection(up, DIFF_W)


_E_KR, _E_UP = _cache_selections()


def _rope_tables(n_lat):
    t = np.arange(n_lat)
    row = (t // GRID_W).astype(np.float64)
    col = (t % GRID_W).astype(np.float64)
    quarter = MLA_ROPE // 4
    inv = ROPE_BASE ** (-np.arange(quarter, dtype=np.float64) / quarter)
    inv = inv.astype(np.float32).astype(np.float64)
    ang = np.concatenate([row[:, None] * inv, row[:, None] * inv, col[:, None] * inv, col[:, None] * inv], axis=1)
    ang = ang.astype(np.float32)
    cos32 = np.cos(ang).astype(np.float32)
    sign = np.concatenate([-np.ones(quarter), np.ones(quarter)] * 2).astype(np.float32)
    sin32 = np.sin(ang).astype(np.float32) * sign
    one, zero = np.ones((n_lat, 32), np.float32), np.zeros((n_lat, 32), np.float32)
    cm = np.concatenate([one, one, cos32, one], axis=1)
    sm = np.concatenate([zero, zero, sin32, zero], axis=1)
    cd = np.concatenate([cos32] * 4, axis=1)
    sd = np.concatenate([sin32] * 4, axis=1)
    return cm, sm, cd, sd


def _mod_kernel(cond_ref, w_ref, b_ref, q1_ref, k1_ref, q2_ref, k2_ref, li_ref, mod_ref, lam_ref):
    a = cond_ref[...]
    a = a * jax.nn.sigmoid(a)
    acc = jnp.dot(a.astype(BF16), w_ref[...].astype(BF16), preferred_element_type=F32)
    mod_ref[...] = acc + b_ref[...]
    d1 = jnp.sum(q1_ref[...] * k1_ref[...], axis=-1, keepdims=True)
    d2 = jnp.sum(q2_ref[...] * k2_ref[...], axis=-1, keepdims=True)
    lam_ref[...] = jnp.exp(d1) - jnp.exp(d2) + li_ref[...]


def _modulation_call(cond, w_ada, b_ada, lam_q1, lam_k1, lam_q2, lam_k2, lam_init_rows):
    depth = w_ada.shape[0]
    n_sec = w_ada.shape[2] // D_MODEL
    rows = cond.shape[0]
    vec = lambda a: a.reshape(depth, 1, a.shape[-1])
    lam_spec = pl.BlockSpec((None, 1, DIFF_DH), lambda l, s: (l, 0, 0))
    return pl.pallas_call(
        _mod_kernel,
        grid=(depth, n_sec),
        in_specs=[
            pl.BlockSpec((rows, D_MODEL), lambda l, s: (0, 0)),
            pl.BlockSpec((None, D_MODEL, D_MODEL), lambda l, s: (l, 0, s)),
            pl.BlockSpec((None, 1, D_MODEL), lambda l, s: (l, 0, s)),
            lam_spec, lam_spec, lam_spec, lam_spec,
            pl.BlockSpec((None, 1, LANE), lambda l, s: (l, 0, 0)),
        ],
        out_specs=[
            pl.BlockSpec((None, rows, D_MODEL), lambda l, s: (l, 0, s)),
            pl.BlockSpec((None, 1, LANE), lambda l, s: (l, 0, 0)),
        ],
        out_shape=[
            jax.ShapeDtypeStruct((depth, rows, n_sec * D_MODEL), F32),
            jax.ShapeDtypeStruct((depth, 1, LANE), F32),
        ],
        compiler_params=_cparams(("arbitrary", "arbitrary")),
        name="modulation",
    )(cond, w_ada, vec(b_ada), vec(lam_q1), vec(lam_k1), vec(lam_q2), vec(lam_k2), lam_init_rows)


def _rms(x):
    return x * lax.rsqrt(jnp.mean(x * x, axis=-1, keepdims=True) + EPS)


def _swap_halves(x):
    lane = lax.broadcasted_iota(jnp.int32, x.shape, 1)
    first = (lane & 8) == 0
    return jnp.where(first, pltpu.roll(x, LANE - 8, axis=1), pltpu.roll(x, 8, axis=1))


def _rope(x, c, s):
    return x * c + _swap_halves(x) * s


def _store_lane_blocks(ref, x):
    for k in range(ref.shape[0]):
        ref[k] = x[:, k * LANE:(k + 1) * LANE]


def _strided_rows(ref, start_of_group, stride, groups):
    return jnp.concatenate(
        [jnp.concatenate([ref[k, pl.ds(start_of_group(j), 8, stride=stride), :] for k in range(ref.shape[0])], axis=1)
         for j in range(groups)], axis=0)


def _inproj_kernel(*refs, rope, ctx, tiles_per_seq):
    it = iter(refs)
    l_ref = next(it)
    x_ref, shift_ref, scale_ref, g_ref, w1_ref, gkv_ref, w2_ref, cs_ref = (next(it) for _ in range(8))
    if rope:
        cm_ref, sm_ref, cd_ref, sd_ref = (next(it) for _ in range(4))
    if ctx:
        for _ in range(4):
            next(it)
    qm_ref, km_ref, qd_ref, kd_ref, vm_ref, vd_ref, ab_ref = (next(it) for _ in range(7))
    if ctx:
        ckv_o, kr_o, kdc_o, vdc_o = (next(it) for _ in range(4))
    h_scr = next(it)
    del l_ref
    tm = x_ref.shape[0]
    upper = lax.broadcasted_iota(jnp.int32, (tm, LANE), 1) >= LANE // 2
    ones_hi = upper.astype(F32)

    x = x_ref[...]
    h = _rms(x) * g_ref[...] * (1.0 + scale_ref[...]) + shift_ref[...]
    h_scr[...] = h.astype(BF16)

    if rope:
        r0 = pl.multiple_of((pl.program_id(0) % tiles_per_seq) * tm, tm)
        cm, sm = cm_ref[pl.ds(r0, tm), :], sm_ref[pl.ds(r0, tm), :]
        cd, sd = cd_ref[pl.ds(r0, tm), :], sd_ref[pl.ds(r0, tm), :]

    def proj(c0, width):
        return jnp.dot(h_scr[...], w1_ref[:, c0:c0 + width], preferred_element_type=F32)

    acc = proj(_C_CKV, 2 * LANE)
    ckvn = _rms(acc[:, :LANE]) * gkv_ref[...]
    kr = acc[:, LANE:]
    def to_cache(ref, val):
        rows = ref.shape[1]
        for s in range(ref.shape[0]):
            ref[s] = val[s * rows:(s + 1) * rows]

    if ctx:
        to_cache(ckv_o, ckvn)
    if rope:
        kr = _rope(kr, cm, sm)
    kv2 = jnp.dot(ckvn.astype(BF16), w2_ref[...], preferred_element_type=F32)
    for hd in range(HEADS):
        blk = slice(hd * LANE, (hd + 1) * LANE)
        km_ref[:, blk] = (kv2[:, blk] + kr).astype(BF16)
        vm_ref[:, blk] = (kv2[:, HW + hd * LANE:HW + (hd + 1) * LANE] + ones_hi).astype(BF16)

    for j in range(HW // (2 * LANE)):
        acc = proj(_C_QM + j * 2 * LANE, 2 * LANE)
        for k in range(2):
            blk = acc[:, k * LANE:(k + 1) * LANE]
            if rope:
                blk = _rope(blk, cm, sm)
            col = (2 * j + k) * LANE
            qm_ref[:, col:col + LANE] = (blk * (MLA_SCALE * LOG2E)).astype(BF16)

    qd_c, kd_c, vd_c = proj(_C_QD, DIFF_W), proj(_C_KD, DIFF_W), proj(_C_VD, DIFF_W)
    if ctx:
        to_cache(kdc_o, kd_c)
        to_cache(vdc_o, vd_c)
    swap = lambda a: pltpu.roll(a, LANE // 2, axis=1)
    for j in range(DIFF_W // LANE):
        pair = slice(j * LANE, (j + 1) * LANE)
        blk_a, blk_b = slice(2 * j * LANE, (2 * j + 1) * LANE), slice((2 * j + 1) * LANE, (2 * j + 2) * LANE)
        q = qd_c[:, pair] * (DIFF_SCALE * LOG2E)
        k = kd_c[:, pair]
        v = vd_c[:, pair]
        q_rot, k_rot = (_rope(q, cd, sd), _rope(k, cd, sd)) if rope else (q, k)
        qd_ref[:, blk_a] = jnp.where(upper, swap(q), q_rot).astype(BF16)
        qd_ref[:, blk_b] = jnp.where(upper, q, swap(q_rot)).astype(BF16)
        kd_ref[:, blk_a] = jnp.where(upper, 0.0, k_rot).astype(BF16)
        kd_ref[:, blk_b] = jnp.where(upper, 0.0, swap(k_rot)).astype(BF16)
        vd_ref[:, blk_a] = jnp.where(upper, swap(v), 1.0).astype(BF16)
        vd_ref[:, blk_b] = jnp.where(upper, v, 1.0).astype(BF16)

    uf = proj(_C_UF, F_WIDTH)
    ab_ref[...] = jnp.dot(uf.astype(BF16), cs_ref[...], preferred_element_type=F32).astype(BF16)

    if ctx:
        to_cache(kr_o, proj(_C_XKR, LANE)[:, :MLA_ROPE])


def _inproj_call(l_arr, x, mod5, b_row, g_mix, w1, g_kv, w2, cs64, tables, seq_len, caches, tm):
    t = x.shape[0]
    n_tiles = t // tm
    tiles_per_seq = max(1, seq_len // tm)
    seqs_per_tile, rows = max(1, tm // seq_len), min(tm, seq_len)
    assert tm % rows == 0 and seq_len % rows == 0
    rope = tables is not None
    ctx = caches is not None
    assert not rope or seqs_per_tile == 1
    n1 = _C_XKR + LANE if ctx else _C_XKR
    row = lambda w: pl.BlockSpec((tm, w), lambda i, l: (i, 0))
    modspec = lambda sec: pl.BlockSpec((None, None, None, 1, D_MODEL),
                                       lambda i, l: (l[0], b_row(i, tiles_per_seq), sec, 0, 0))
    in_specs = [
        row(D_MODEL), modspec(0), modspec(1),
        pl.BlockSpec((None, 1, D_MODEL), lambda i, l: (l[0], 0, 0)),
        pl.BlockSpec((None, D_MODEL, n1), lambda i, l: (l[0], 0, 0)),
        pl.BlockSpec((None, 1, MLA_RANK), lambda i, l: (l[0], 0, 0)),
        pl.BlockSpec((None, MLA_RANK, 2 * HW), lambda i, l: (l[0], 0, 0)),
        pl.BlockSpec((F_WIDTH, 2 * F_WIDTH), lambda i, l: (0, 0)),
    ]
    args = [x, mod5, mod5, g_mix, w1, g_kv, w2, cs64]
    if rope:
        in_specs += [pl.BlockSpec((seq_len, LANE), lambda i, l: (0, 0))] * 4
        args += list(tables)
    out_specs = [row(HW)] * 6 + [row(2 * F_WIDTH)]
    out_shape = [jax.ShapeDtypeStruct((t, HW), BF16)] * 6 + [jax.ShapeDtypeStruct((t, 2 * F_WIDTH), BF16)]
    aliases = {}
    if ctx:
        for k, a in enumerate(caches):
            aliases[1 + len(args)] = len(out_shape)
            in_specs.append(pl.BlockSpec(memory_space=pl.ANY))
            args.append(a)
            out_specs.append(pl.BlockSpec((seqs_per_tile, None, rows, a.shape[-1]),
                                          lambda i, l: (i // tiles_per_seq, l[0], i % tiles_per_seq, 0)))
            out_shape.append(jax.ShapeDtypeStruct(a.shape, a.dtype))
    return pl.pallas_call(
        functools.partial(_inproj_kernel, rope=rope, ctx=ctx, tiles_per_seq=tiles_per_seq),
        grid_spec=pltpu.PrefetchScalarGridSpec(
            num_scalar_prefetch=1, grid=(n_tiles,), in_specs=in_specs, out_specs=out_specs,
            scratch_shapes=[pltpu.VMEM((tm, D_MODEL), BF16)]),
        out_shape=out_shape,
        input_output_aliases=aliases,
        compiler_params=_cparams(("arbitrary",)),
        name="inproj_ctx" if ctx else "inproj_lat",
    )(l_arr, *args)


def _cacheprep_kernel(ckv_ref, kr_ref, kd_ref, vd_ref, w2_ref, ekr_ref, eup_ref,
                      km_ref, kdo_ref, vm_ref, vdo_ref):
    dot = lambda a, b: jnp.dot(a.astype(BF16), b, preferred_element_type=F32)
    kv2 = dot(ckv_ref[...], w2_ref[...])
    ones_hi = ((lax.broadcasted_iota(jnp.int32, (1, HW), 1) % LANE) >= LANE // 2).astype(F32)
    km_ref[...] = (kv2[:, :HW] + dot(kr_ref[...], ekr_ref[...])).astype(BF16)
    kdo_ref[...] = dot(kd_ref[...], eup_ref[...]).astype(BF16)
    vm_ref[...] = (kv2[:, HW:] + ones_hi).astype(BF16)
    vdo_ref[...] = (dot(vd_ref[...], eup_ref[...]) + (1.0 - ones_hi)).astype(BF16)


def _cacheprep_call(cache_ckv, cache_kr, cache_kd, cache_vd, w2, ekr, eup):
    b, depth, p = cache_ckv.shape[:3]
    cspec = lambda w: pl.BlockSpec((None, None, p, w), lambda l, i: (i, l, 0, 0))
    const = lambda a: pl.BlockSpec(a.shape, lambda l, i: (0,) * a.ndim)
    ospec = pl.BlockSpec((None, None, p, HW), lambda l, i: (l, i, 0, 0))
    oshape = jax.ShapeDtypeStruct((depth, b, p, HW), BF16)
    return pl.pallas_call(
        _cacheprep_kernel,
        grid=(depth, b),
        in_specs=[cspec(MLA_RANK), cspec(MLA_ROPE), cspec(HEADS * 2 * DIFF_DH), cspec(HEADS * DIFF_DV),
                  pl.BlockSpec((None, MLA_RANK, 2 * HW), lambda l, i: (l, 0, 0)),
                  const(ekr), const(eup)],
        out_specs=[ospec] * 4,
        out_shape=[oshape] * 4,
        compiler_params=_cparams(("arbitrary", "arbitrary")),
        name="cacheprep",
    )(cache_ckv, cache_kr, cache_kd, cache_vd, w2, ekr, eup)


def _softmax_pv(q, k_refs, v_refs, blk):
    nt = (((1,), (1,)), ((), ()))
    s = [lax.dot_general(q, k[:, blk], nt, preferred_element_type=F32) for k in k_refs]
    m = functools.reduce(jnp.maximum, [jnp.max(si, axis=-1, keepdims=True) for si in s])
    o = functools.reduce(jnp.add, [jnp.dot(jnp.exp2(si - m).astype(BF16), v[:, blk], preferred_element_type=F32)
                                   for si, v in zip(s, v_refs)])
    return o / pltpu.roll(o, LANE // 2, axis=1)


def _attn_kernel(*refs, n_seg, heads, tq, phased):
    it = iter(refs)
    l_ref = next(it)
    lam_ref, gsub_ref, sub_ref, qm_ref, qd_ref = (next(it) for _ in range(5))
    km = [next(it) for _ in range(n_seg)]
    kd = [next(it) for _ in range(n_seg)]
    vm = [next(it) for _ in range(n_seg)]
    vd = [next(it) for _ in range(n_seg)]
    o_ref = next(it)
    del l_ref

    sub = min(tq, TQ_SUB)
    lane = lax.broadcasted_iota(jnp.int32, (sub, LANE), 1)
    upper = lane >= LANE // 2
    first = (lane // DIFF_DH) % 2 == 0
    nt = (((1,), (1,)), ((), ()))
    problems = [(bi, slice(hd * LANE, (hd + 1) * LANE), slice(r * sub, (r + 1) * sub))
                for bi in range(qm_ref.shape[0]) for hd in range(heads) for r in range(tq // sub)]
    at = lambda refs, bi: [r.at[bi] for r in refs]

    if phased:
        qs, ks, vs, lanes = [], [], [], []
        for bi, blk, rows in problems:
            qd = qd_ref[bi, rows, blk]
            zero = jnp.zeros_like(qd)
            qs += [qm_ref[bi, rows, blk], jnp.where(first, qd, zero), jnp.where(first, zero, qd)]
            ks += [at(km, bi), at(kd, bi), at(kd, bi)]
            vs += [at(vm, bi), at(vd, bi), at(vd, bi)]
            lanes += [blk] * 3
        scores = [[lax.dot_general(q, k[:, blk], nt, preferred_element_type=F32) for k in kk]
                  for q, kk, blk in zip(qs, ks, lanes)]
        maxes = [functools.reduce(jnp.maximum, [jnp.max(si, axis=-1, keepdims=True) for si in s]) for s in scores]
        probs = [[jnp.exp2(si - m).astype(BF16) for si in s] for s, m in zip(scores, maxes)]
        outs = [functools.reduce(jnp.add, [jnp.dot(pi, v[:, blk], preferred_element_type=F32)
                                           for pi, v in zip(p, vv)])
                for p, vv, blk in zip(probs, vs, lanes)]
        outs = [o / pltpu.roll(o, LANE // 2, axis=1) for o in outs]
    else:
        outs = []
        for bi, blk, rows in problems:
            qd = qd_ref[bi, rows, blk]
            zero = jnp.zeros_like(qd)
            outs.append(_softmax_pv(qm_ref[bi, rows, blk], at(km, bi), at(vm, bi), blk))
            outs.append(_softmax_pv(jnp.where(first, qd, zero), at(kd, bi), at(vd, bi), blk))
            outs.append(_softmax_pv(jnp.where(first, zero, qd), at(kd, bi), at(vd, bi), blk))

    for n, (bi, blk, rows) in enumerate(problems):
        o_m, o_1, o_2 = outs[3 * n:3 * n + 3]
        o_d = jnp.where(upper, o_1 - lam_ref[...] * o_2, 0.0)
        ms = jnp.sum(o_d * o_d, axis=-1, keepdims=True) * (1.0 / DIFF_DV)
        o_d = o_d * lax.rsqrt(ms + EPS) * gsub_ref[...] * sub_ref[...]
        o_ref[bi, rows, blk] = jnp.where(upper, o_d, o_m).astype(BF16)


def _attn_call(l_arr, lam, gsub, sub, qm, qd, segs, batch, n_q, tq, heads, bb):
    n_seg = len(segs)
    width = heads * LANE
    q3 = lambda a: a.reshape(batch, n_q, HW)
    layer_vec = pl.BlockSpec((None, 1, LANE), lambda b, h, i, l: (l[0], 0, 0))
    qspec = pl.BlockSpec((bb, tq, width), lambda b, h, i, l: (b, i, h))
    kspecs, kargs, n_keys = [], [], 0
    for which in range(4):
        for seg in segs:
            a = seg[which]
            if a.ndim == 4:
                kspecs.append(pl.BlockSpec((None, bb, a.shape[2], width), lambda b, h, i, l: (l[0], b, 0, h)))
            else:
                kspecs.append(pl.BlockSpec((bb, a.shape[1], width), lambda b, h, i, l: (b, 0, h)))
            kargs.append(a)
            n_keys += a.shape[-2] if which == 0 else 0
    return pl.pallas_call(
        functools.partial(_attn_kernel, n_seg=n_seg, heads=heads, tq=tq, phased=n_keys <= PHASED_MAX_KEYS),
        grid_spec=pltpu.PrefetchScalarGridSpec(
            num_scalar_prefetch=1, grid=(batch // bb, HEADS // heads, n_q // tq),
            in_specs=[layer_vec, layer_vec, layer_vec, qspec, qspec] + kspecs,
            out_specs=qspec),
        out_shape=jax.ShapeDtypeStruct((batch, n_q, HW), BF16),
        compiler_params=_cparams(("arbitrary", "arbitrary", "arbitrary")),
        name="attention_%dseg" % n_seg,
    )(l_arr, lam, gsub, sub, q3(qm), q3(qd), *kargs).reshape(batch * n_q, HW)


def _dft_kernel(c_ref, s_ref, ab_ref, y_ref):
    for bi in range(ab_ref.shape[0]):
        ya = jnp.dot(c_ref[...], ab_ref[bi, :, :F_WIDTH], preferred_element_type=F32)
        yb = jnp.dot(s_ref[...], ab_ref[bi, :, F_WIDTH:], preferred_element_type=F32)
        y_ref[bi] = (ya - yb).astype(BF16)


def _dft_call(cn, sn, ab, batch, n):
    tr = min(n, TM)
    bb = math.gcd(batch, max(1, DFT_ROWS_PER_STEP // n))
    mat = pl.BlockSpec((tr, n), lambda b, r: (r, 0))
    return pl.pallas_call(
        _dft_kernel,
        grid=(batch // bb, n // tr),
        in_specs=[mat, mat, pl.BlockSpec((bb, n, 2 * F_WIDTH), lambda b, r: (b, 0, 0))],
        out_specs=pl.BlockSpec((bb, tr, F_WIDTH), lambda b, r: (b, r, 0)),
        out_shape=jax.ShapeDtypeStruct((batch, n, F_WIDTH), BF16),
        compiler_params=_cparams(("arbitrary", "arbitrary")),
        name="position_dft",
    )(cn, sn, ab.reshape(batch, n, 2 * F_WIDTH)).reshape(batch * n, F_WIDTH)


def _mixffn_kernel(l_ref, o_ref, op_ref, on_ref, y_ref, yp_ref, yn_ref, x_ref, xp_ref, xn_ref,
                   ga_ref, sf_ref, cf_ref, gf_ref, g_ref, gfin_ref, wa_ref, wf_ref, wup_ref, wconv_ref, wdn_ref,
                   xo_ref, lo_ref, ly_ref, hs_ref, hx_ref, act_ref, ys_ref, *, tiles_per_seq, final):
    del l_ref
    tm = x_ref.shape[0]
    groups = tm // 8
    i = pl.program_id(0)
    has_prev = (i % tiles_per_seq) != 0
    has_next = (i % tiles_per_seq) != tiles_per_seq - 1
    for dst, parts in ((lo_ref, (o_ref, op_ref, on_ref)), (ly_ref, (y_ref, yp_ref, yn_ref))):
        dst[0:tm, :] = parts[0][...]
        dst[tm:tm + HALO, :] = parts[1][...]
        dst[tm + HALO:, :] = parts[2][...]
    attn = jnp.dot(lo_ref[...], wa_ref[...], preferred_element_type=F32)
    attn = attn + jnp.dot(ly_ref[...], wf_ref[...], preferred_element_type=F32)
    x = jnp.concatenate([x_ref[...], xp_ref[...], xn_ref[...]], axis=0) + ga_ref[...] * attn
    xo_ref[...] = x[:tm]
    h = _rms(x) * g_ref[...] * (1.0 + cf_ref[...]) + sf_ref[...]
    mlp_row = lambda j: 8 * ((8 * j) % groups) + (8 * j) // groups
    for j in range(groups):
        for k in range(D_MODEL // LANE):
            hs_ref[k, pl.ds(mlp_row(j), 8, stride=8), :] = h[8 * j:8 * j + 8, k * LANE:(k + 1) * LANE]
    hx_ref[0:tm, :] = jnp.concatenate([hs_ref[k] for k in range(D_MODEL // LANE)], axis=1).astype(BF16)
    zero = jnp.zeros((HALO, D_MODEL), F32)
    hx_ref[tm:tm + HALO, :] = jnp.where(has_prev, h[tm:tm + HALO], zero).astype(BF16)
    hx_ref[tm + HALO:, :] = jnp.where(has_next, h[tm + HALO:], zero).astype(BF16)

    for c in range(N_FF_CHUNKS):
        halves = []
        for cols in (slice(c * FF_CHUNK, (c + 1) * FF_CHUNK),
                     slice(D_FF + c * FF_CHUNK, D_FF + (c + 1) * FF_CHUNK)):
            u = jnp.dot(hx_ref[...], wup_ref[:, cols], preferred_element_type=F32)
            wc = wconv_ref[:, cols]
            mid = u[:tm]
            wrap_dn = jnp.concatenate([u[tm + HALO - 1:tm + HALO], mid[tm - 8:tm - 1]], axis=0)
            wrap_up = jnp.concatenate([mid[1:8], u[tm + HALO:tm + HALO + 1]], axis=0)
            dn = jnp.concatenate([wrap_dn, mid[:tm - 8]], axis=0)
            up = jnp.concatenate([mid[8:], wrap_up], axis=0)
            halves.append(dn * wc[0:1, :] + mid * wc[1:2, :] + up * wc[2:3, :])
        gate, val = halves
        act_ref[:, c * FF_CHUNK:(c + 1) * FF_CHUNK] = (gate * jax.nn.sigmoid(gate) * val).astype(BF16)

    _store_lane_blocks(ys_ref, jnp.dot(act_ref[...], wdn_ref[...], preferred_element_type=F32))
    y = _strided_rows(ys_ref, mlp_row, 8, groups)
    x = xo_ref[...] + gf_ref[...] * y
    if final:
        x = _rms(x) * gfin_ref[...]
    xo_ref[...] = x


def _mixffn_call(l_arr, o, y, x, mod5, b_row, g_ffn, g_final, wo_a, wo_f, wup, wconv, wdn, seq_len, final, tm):
    t = x.shape[0]
    n_tiles = t // tm
    per = tm // HALO
    ext = tm + 2 * HALO

    def with_halos(w):
        return [pl.BlockSpec((tm, w), lambda i, l: (i, 0)),
                pl.BlockSpec((HALO, w), lambda i, l: (jnp.maximum(i * per - 1, 0), 0)),
                pl.BlockSpec((HALO, w), lambda i, l: (jnp.minimum((i + 1) * per, n_tiles * per - 1), 0))]

    tiles_per_seq = seq_len // tm
    modspec = lambda sec: pl.BlockSpec((None, None, None, 1, D_MODEL),
                                       lambda i, l: (l[0], b_row(i, tiles_per_seq), sec, 0, 0))
    whole = lambda a: pl.BlockSpec((None,) + a.shape[1:], lambda i, l: (l[0],) + (0,) * (a.ndim - 1))
    return pl.pallas_call(
        functools.partial(_mixffn_kernel, tiles_per_seq=tiles_per_seq, final=final),
        grid_spec=pltpu.PrefetchScalarGridSpec(
            num_scalar_prefetch=1, grid=(n_tiles,),
            in_specs=with_halos(HW) + with_halos(F_WIDTH) + with_halos(D_MODEL)
            + [modspec(2), modspec(3), modspec(4), modspec(5), whole(g_ffn),
               pl.BlockSpec((1, D_MODEL), lambda i, l: (0, 0)),
               whole(wo_a), whole(wo_f), whole(wup), whole(wconv), whole(wdn)],
            out_specs=pl.BlockSpec((tm, D_MODEL), lambda i, l: (i, 0)),
            scratch_shapes=[pltpu.VMEM((ext, HW), BF16), pltpu.VMEM((ext, F_WIDTH), BF16),
                            pltpu.VMEM((D_MODEL // LANE, tm, LANE), F32), pltpu.VMEM((ext, D_MODEL), BF16),
                            pltpu.VMEM((tm, D_FF), BF16), pltpu.VMEM((D_MODEL // LANE, tm, LANE), F32)]),
        out_shape=jax.ShapeDtypeStruct((t, D_MODEL), F32),
        compiler_params=_cparams(("arbitrary",)),
        name="mix_ffn_final" if final else "mix_ffn",
    )(l_arr, o, o, o, y, y, y, x, x, x, mod5, mod5, mod5, mod5, g_ffn, g_final, wo_a, wo_f, wup, wconv, wdn)


def _dft_mats(n, inner=64):
    j = jnp.arange(n, dtype=jnp.int32)[:, None]
    ang = lambda k: ((j * k[None, :]) % n).astype(F32) * (2.0 * math.pi / n)
    a = ang(jnp.arange(n // inner, dtype=jnp.int32) * inner)[:, :, None]
    b = ang(jnp.arange(inner, dtype=jnp.int32))[:, None, :]
    s = 1.0 / math.sqrt(n)
    ca, sa, cb, sb = jnp.cos(a) * s, jnp.sin(a) * s, jnp.cos(b), jnp.sin(b)
    return ((ca * cb - sa * sb).reshape(n, n).astype(BF16), (sa * cb + ca * sb).reshape(n, n).astype(BF16))


def _channel_dft():
    k = np.arange(F_GROUP_W)
    ang = 2.0 * np.pi * ((k[:, None] * k[None, :]) % F_GROUP_W) / F_GROUP_W
    eye = np.eye(F_GROUPS)
    s = 1.0 / math.sqrt(F_GROUP_W)
    return np.concatenate([np.kron(eye, np.cos(ang) * s), np.kron(eye, np.sin(ang) * s)], axis=1).astype(np.float32)


def kernel(x_prompt, x_sample, c, cache_mla_ckv, cache_mla_krope, cache_diff_k, cache_diff_v, c_ctx, w_ada, b_ada,
           g_mix_norm, g_ffn_norm, w_in, g_kv_norm, w_uk, w_uv, lam_q1, lam_k1, lam_q2, lam_k2, g_diff_subln,
           w_out, w_up, w_conv, w_down, g_final):
    batch_c, seq_c, _ = x_prompt.shape
    batch_l, seq_l, _ = x_sample.shape
    depth = w_in.shape[0]
    assert seq_c % TM == 0 and seq_l % TM == 0 and seq_l % GRID_W == 0 and D_FF % FF_CHUNK == 0
    assert seq_c % TQ_CTX == 0 and seq_l % TQ_LAT == 0 and batch_c % BB_CTX == 0 and seq_l % TM_IN_LAT == 0
    assert seq_l % TM_LAT == 0 and (batch_c * seq_c) % TM_IN_CTX == 0 and HEADS % HEADS_LAT == 0

    w1 = _take(w_in.astype(BF16), _W1_COLS)
    w2 = _take(jnp.concatenate([w_uk, w_uv], axis=-1).astype(BF16), _W2_COLS)
    wo_a = _take(w_out.astype(BF16), _WOUT_ROWS, axis=1)
    wo_f = w_out[:, HEADS * (MLA_DV + DIFF_DV):, :].astype(BF16)
    wup, wconv, wdn = w_up.astype(BF16), w_conv, w_down.astype(BF16)
    vec = lambda a: a.reshape(depth, 1, a.shape[-1])
    g_mix, g_ffn, g_kv = vec(g_mix_norm), vec(g_ffn_norm), vec(g_kv_norm)
    gsub = jnp.concatenate([jnp.ones_like(g_diff_subln), g_diff_subln], axis=-1).reshape(depth, 1, LANE)
    lam_init = np.asarray([0.8 - 0.6 * math.exp(-0.3 * l) for l in range(depth)], np.float32)
    lam_init_rows = jnp.asarray(np.broadcast_to(lam_init[:, None, None], (depth, 1, LANE)))
    sub_rows = jnp.asarray(np.broadcast_to((1.0 - lam_init)[:, None, None], (depth, 1, LANE)).astype(np.float32))
    g_fin = g_final.reshape(1, D_MODEL)
    cs64 = jnp.asarray(_channel_dft()).astype(BF16)
    tables = tuple(jnp.asarray(a) for a in _rope_tables(seq_l))
    dft_c, dft_l = _dft_mats(seq_c), _dft_mats(seq_l)

    n_rows = -(-(batch_l + 1) // 8) * 8
    cond = jnp.zeros((n_rows, D_MODEL), F32).at[:batch_l].set(c).at[batch_l].set(c_ctx)
    mod, lam = _modulation_call(cond, w_ada, b_ada, lam_q1, lam_k1, lam_q2, lam_k2, lam_init_rows)
    mod5 = mod.reshape(depth, n_rows, w_ada.shape[2] // D_MODEL, 1, D_MODEL)
    row_l = lambda i, tiles_per_seq: i // tiles_per_seq
    row_c = lambda i, tiles_per_seq: batch_l

    flat = lambda a: a.reshape(a.shape[0], a.shape[1], a.shape[2], -1)
    past_kv = _cacheprep_call(cache_mla_ckv, cache_mla_krope, flat(cache_diff_k), flat(cache_diff_v), w2,
                              jnp.asarray(_E_KR).astype(BF16), jnp.asarray(_E_UP).astype(BF16))

    x_c = x_prompt.reshape(batch_c * seq_c, D_MODEL)
    x_l = x_sample.reshape(batch_l * seq_l, D_MODEL)
    caches = tuple(jnp.zeros((batch_c, depth, seq_c, w), F32)
                   for w in (MLA_RANK, MLA_ROPE, HEADS * 2 * DIFF_DH, HEADS * DIFF_DV))
    for l in range(depth):
        l_arr = jnp.full((1,), l, jnp.int32)
        final = l == depth - 1
        qm, km, qd, kd, vm, vd, ab, *caches = _inproj_call(
            l_arr, x_c, mod5, row_c, g_mix, w1, g_kv, w2, cs64, None, seq_c, caches, TM_IN_CTX)
        k3 = lambda a: a.reshape(batch_c, seq_c, HW)
        o = _attn_call(l_arr, lam, gsub, sub_rows, qm, qd, [(k3(km), k3(kd), k3(vm), k3(vd))],
                       batch_c, seq_c, TQ_CTX, HEADS, BB_CTX)
        y = _dft_call(*dft_c, ab, batch_c, seq_c)
        x_c = _mixffn_call(l_arr, o, y, x_c, mod5, row_c, g_ffn, g_fin, wo_a, wo_f, wup, wconv, wdn, seq_c, final,
                           TM)
        qm, km, qd, kd, vm, vd, ab = _inproj_call(
            l_arr, x_l, mod5, row_l, g_mix, w1, g_kv, w2, cs64, tables, seq_l, None, TM_IN_LAT)
        k3 = lambda a: a.reshape(batch_l, seq_l, HW)
        o = _attn_call(l_arr, lam, gsub, sub_rows, qm, qd, [(k3(km), k3(kd), k3(vm), k3(vd)), past_kv],
                       batch_l, seq_l, TQ_LAT, HEADS_LAT, 1)
        y = _dft_call(*dft_l, ab, batch_l, seq_l)
        x_l = _mixffn_call(l_arr, o, y, x_l, mod5, row_l, g_ffn, g_fin, wo_a, wo_f, wup, wconv, wdn, seq_l, final,
                           TM_LAT)

    new_ckv, new_kr, new_kd, new_vd = caches
    return (x_c.reshape(batch_c, seq_c, D_MODEL),
            x_l.reshape(batch_l, seq_l, D_MODEL),
            new_ckv, new_kr,
            new_kd.reshape(batch_c, depth, seq_c, HEADS, 2 * DIFF_DH),
            new_vd.reshape(batch_c, depth, seq_c, HEADS, DIFF_DV))
```

```python
import functools
import math

import numpy as np
import jax
import jax.numpy as jnp
from jax import lax
from jax.experimental import pallas as pl
from jax.experimental.pallas import tpu as pltpu

F32 = jnp.float32
BF16 = jnp.bfloat16

D_MODEL = 1024
GRID_W = 64
HEADS = 6
MLA_NOPE, MLA_ROPE, MLA_DV, MLA_RANK = 64, 32, 64, 128
DIFF_DH, DIFF_DV = 32, 64
F_GROUPS, F_GROUP_W = 4, 64
F_WIDTH = F_GROUPS * F_GROUP_W
D_FF = 2816
ROPE_BASE = 10000.0
EPS = 1e-6
MLA_SCALE = (MLA_NOPE + MLA_ROPE) ** -0.5
DIFF_SCALE = DIFF_DH ** -0.5
LOG2E = math.log2(math.e)

_QM0 = 0
_CKV0 = HEADS * (MLA_NOPE + MLA_ROPE)
_KR0 = _CKV0 + MLA_RANK
_QD0 = _KR0 + MLA_ROPE
_KD0 = _QD0 + HEADS * 2 * DIFF_DH
_VD0 = _KD0 + HEADS * 2 * DIFF_DH
_UF0 = _VD0 + HEADS * DIFF_DV

LANE = 128
HW = HEADS * LANE
TM = 256
TM_CTX = 512
TM_LAT = 512
TM_IN_LAT = 512
TM_IN_CTX = 512
TQ_CTX = 256
TQ_LAT = 1024
HEADS_LAT = 2
TQ_SUB = 512
BB_CTX = 4
DFT_ROWS_PER_STEP = 8192
PHASED_MAX_KEYS = 512
FF_CHUNK = 256
N_FF_CHUNKS = D_FF // FF_CHUNK
HALO = 8
VMEM_LIMIT = 56 * 1024 * 1024


def _cparams(sem):
    return pltpu.CompilerParams(dimension_semantics=sem, vmem_limit_bytes=VMEM_LIMIT)


def _w1_columns():
    z = lambda n: [-1] * n
    r = lambda a, n: list(range(a, a + n))
    cols = []
    cols += r(_CKV0, MLA_RANK)
    cols += z(MLA_NOPE) + r(_KR0, MLA_ROPE) + z(LANE - MLA_NOPE - MLA_ROPE)
    for h in range(HEADS):
        b = _QM0 + h * (MLA_NOPE + MLA_ROPE)
        cols += r(b, MLA_NOPE) + r(b + MLA_NOPE, MLA_ROPE) + r(b + MLA_NOPE, MLA_ROPE)
    cols += r(_QD0, _UF0 + F_WIDTH - _QD0)
    cols += r(_KR0, MLA_ROPE) + z(LANE - MLA_ROPE)
    return np.asarray(cols, np.int32)


_W1_COLS = _w1_columns()
DIFF_W = HEADS * 2 * DIFF_DH
assert DIFF_W == HEADS * DIFF_DV and 2 * DIFF_DH == LANE // 2
_C_CKV, _C_KR, _C_QM = 0, LANE, 2 * LANE
_C_QD = _C_QM + HW
_C_KD = _C_QD + DIFF_W
_C_VD = _C_KD + DIFF_W
_C_UF = _C_VD + DIFF_W
_C_XKR = _C_UF + F_WIDTH


def _w2_columns():
    cols = []
    for base in (0, HEADS * MLA_NOPE):
        for h in range(HEADS):
            cols += list(range(base + h * 64, base + (h + 1) * 64)) + [-1] * 64
    return np.asarray(cols, np.int32)


_W2_COLS = _w2_columns()


def _wout_rows():
    rows = []
    for h in range(HEADS):
        rows += list(range(h * MLA_DV, (h + 1) * MLA_DV))
        rows += list(range(HEADS * MLA_DV + h * DIFF_DV, HEADS * MLA_DV + (h + 1) * DIFF_DV))
    return np.asarray(rows, np.int32)


_WOUT_ROWS = _wout_rows()


def _take(w, idx, axis=-1):
    axis = axis % w.ndim
    pieces, i, n = [], 0, len(idx)
    while i < n:
        j = i + 1
        if idx[i] < 0:
            while j < n and idx[j] < 0:
                j += 1
            shape = w.shape[:axis] + (j - i,) + w.shape[axis + 1:]
            pieces.append(jnp.zeros(shape, w.dtype))
        else:
            while j < n and idx[j] == idx[j - 1] + 1:
                j += 1
            pieces.append(lax.slice_in_dim(w, int(idx[i]), int(idx[i]) + j - i, axis=axis))
        i = j
    return jnp.concatenate(pieces, axis=axis)


def _selection(src_of_col, n_src):
    e = np.zeros((n_src, len(src_of_col)), np.float32)
    for c, s in enumerate(src_of_col):
        if s >= 0:
            e[s, c] = 1.0
    return e


def _cache_selections():
    kr, up = [], []
    for h in range(HEADS):
        kr += [-1] * (MLA_NOPE + MLA_ROPE) + list(range(MLA_ROPE))
        up += [-1] * (LANE // 2) + list(range(h * LANE // 2, (h + 1) * LANE // 2))
    return _selection(kr, MLA_ROPE), _selection(up, DIFF_W)


_E_KR, _E_UP = _cache_selections()


def _rope_tables(n_lat):
    t = np.arange(n_lat)
    row = (t // GRID_W).astype(np.float64)
    col = (t % GRID_W).astype(np.float64)
    quarter = MLA_ROPE // 4
    inv = ROPE_BASE ** (-np.arange(quarter, dtype=np.float64) / quarter)
    inv = inv.astype(np.float32).astype(np.float64)
    ang = np.concatenate([row[:, None] * inv, row[:, None] * inv, col[:, None] * inv, col[:, None] * inv], axis=1)
    ang = ang.astype(np.float32)
    cos32 = np.cos(ang).astype(np.float32)
    sign = np.concatenate([-np.ones(quarter), np.ones(quarter)] * 2).astype(np.float32)
    sin32 = np.sin(ang).astype(np.float32) * sign
    one, zero = np.ones((n_lat, 32), np.float32), np.zeros((n_lat, 32), np.float32)
    cm = np.concatenate([one, one, cos32, one], axis=1)
    sm = np.concatenate([zero, zero, sin32, zero], axis=1)
    cd = np.concatenate([cos32] * 4, axis=1)
    sd = np.concatenate([sin32] * 4, axis=1)
    return cm, sm, cd, sd


def _mod_kernel(cond_ref, w_ref, b_ref, q1_ref, k1_ref, q2_ref, k2_ref, li_ref, mod_ref, lam_ref):
    a = cond_ref[...]
    a = a * jax.nn.sigmoid(a)
    acc = jnp.dot(a.astype(BF16), w_ref[...].astype(BF16), preferred_element_type=F32)
    mod_ref[...] = acc + b_ref[...]
    d1 = jnp.sum(q1_ref[...] * k1_ref[...], axis=-1, keepdims=True)
    d2 = jnp.sum(q2_ref[...] * k2_ref[...], axis=-1, keepdims=True)
    lam_ref[...] = jnp.exp(d1) - jnp.exp(d2) + li_ref[...]


def _modulation_call(cond, w_ada, b_ada, lam_q1, lam_k1, lam_q2, lam_k2, lam_init_rows):
    depth = w_ada.shape[0]
    n_sec = w_ada.shape[2] // D_MODEL
    rows = cond.shape[0]
    vec = lambda a: a.reshape(depth, 1, a.shape[-1])
    lam_spec = pl.BlockSpec((None, 1, DIFF_DH), lambda l, s: (l, 0, 0))
    return pl.pallas_call(
        _mod_kernel,
        grid=(depth, n_sec),
        in_specs=[
            pl.BlockSpec((rows, D_MODEL), lambda l, s: (0, 0)),
            pl.BlockSpec((None, D_MODEL, D_MODEL), lambda l, s: (l, 0, s)),
            pl.BlockSpec((None, 1, D_MODEL), lambda l, s: (l, 0, s)),
            lam_spec, lam_spec, lam_spec, lam_spec,
            pl.BlockSpec((None, 1, LANE), lambda l, s: (l, 0, 0)),
        ],
        out_specs=[
            pl.BlockSpec((None, rows, D_MODEL), lambda l, s: (l, 0, s)),
            pl.BlockSpec((None, 1, LANE), lambda l, s: (l, 0, 0)),
        ],
        out_shape=[
            jax.ShapeDtypeStruct((depth, rows, n_sec * D_MODEL), F32),
            jax.ShapeDtypeStruct((depth, 1, LANE), F32),
        ],
        compiler_params=_cparams(("arbitrary", "arbitrary")),
        name="modulation",
    )(cond, w_ada, vec(b_ada), vec(lam_q1), vec(lam_k1), vec(lam_q2), vec(lam_k2), lam_init_rows)


def _rms(x):
    return x * lax.rsqrt(jnp.mean(x * x, axis=-1, keepdims=True) + EPS)


def _swap_halves(x):
    lane = lax.broadcasted_iota(jnp.int32, x.shape, 1)
    first = (lane & 8) == 0
    return jnp.where(first, pltpu.roll(x, LANE - 8, axis=1), pltpu.roll(x, 8, axis=1))


def _rope(x, c, s):
    return x * c + _swap_halves(x) * s


def _store_lane_blocks(ref, x):
    for k in range(ref.shape[0]):
        ref[k] = x[:, k * LANE:(k + 1) * LANE]


def _strided_rows(ref, start_of_group, stride, groups):
    return jnp.concatenate(
        [jnp.concatenate([ref[k, pl.ds(start_of_group(j), 8, stride=stride), :] for k in range(ref.shape[0])], axis=1)
         for j in range(groups)], axis=0)


def _inproj_kernel(*refs, rope, ctx, tiles_per_seq):
    it = iter(refs)
    l_ref = next(it)
    x_ref, shift_ref, scale_ref, g_ref, w1_ref, gkv_ref, w2_ref, cs_ref = (next(it) for _ in range(8))
    if rope:
        cm_ref, sm_ref, cd_ref, sd_ref = (next(it) for _ in range(4))
    if ctx:
        for _ in range(4):
            next(it)
    qm_ref, km_ref, qd_ref, kd_ref, vm_ref, vd_ref, ab_ref = (next(it) for _ in range(7))
    if ctx:
        ckv_o, kr_o, kdc_o, vdc_o = (next(it) for _ in range(4))
    h_scr = next(it)
    del l_ref
    tm = x_ref.shape[0]
    upper = lax.broadcasted_iota(jnp.int32, (tm, LANE), 1) >= LANE // 2
    ones_hi = upper.astype(F32)

    x = x_ref[...]
    h = _rms(x) * g_ref[...] * (1.0 + scale_ref[...]) + shift_ref[...]
    h_scr[...] = h.astype(BF16)

    if rope:
        r0 = pl.multiple_of((pl.program_id(0) % tiles_per_seq) * tm, tm)
        cm, sm = cm_ref[pl.ds(r0, tm), :], sm_ref[pl.ds(r0, tm), :]
        cd, sd = cd_ref[pl.ds(r0, tm), :], sd_ref[pl.ds(r0, tm), :]

    def proj(c0, width):
        return jnp.dot(h_scr[...], w1_ref[:, c0:c0 + width], preferred_element_type=F32)

    acc = proj(_C_CKV, 2 * LANE)
    ckvn = _rms(acc[:, :LANE]) * gkv_ref[...]
    kr = acc[:, LANE:]
    def to_cache(ref, val):
        rows = ref.shape[1]
        for s in range(ref.shape[0]):
            ref[s] = val[s * rows:(s + 1) * rows]

    if ctx:
        to_cache(ckv_o, ckvn)
    if rope:
        kr = _rope(kr, cm, sm)
    kv2 = jnp.dot(ckvn.astype(BF16), w2_ref[...], preferred_element_type=F32)
    for hd in range(HEADS):
        blk = slice(hd * LANE, (hd + 1) * LANE)
        km_ref[:, blk] = (kv2[:, blk] + kr).astype(BF16)
        vm_ref[:, blk] = (kv2[:, HW + hd * LANE:HW + (hd + 1) * LANE] + ones_hi).astype(BF16)

    for j in range(HW // (2 * LANE)):
        acc = proj(_C_QM + j * 2 * LANE, 2 * LANE)
        for k in range(2):
            blk = acc[:, k * LANE:(k + 1) * LANE]
            if rope:
                blk = _rope(blk, cm, sm)
            col = (2 * j + k) * LANE
            qm_ref[:, col:col + LANE] = (blk * (MLA_SCALE * LOG2E)).astype(BF16)

    qd_c, kd_c, vd_c = proj(_C_QD, DIFF_W), proj(_C_KD, DIFF_W), proj(_C_VD, DIFF_W)
    if ctx:
        to_cache(kdc_o, kd_c)
        to_cache(vdc_o, vd_c)
    swap = lambda a: pltpu.roll(a, LANE // 2, axis=1)
    for j in range(DIFF_W // LANE):
        pair = slice(j * LANE, (j + 1) * LANE)
        blk_a, blk_b = slice(2 * j * LANE, (2 * j + 1) * LANE), slice((2 * j + 1) * LANE, (2 * j + 2) * LANE)
        q = qd_c[:, pair] * (DIFF_SCALE * LOG2E)
        k = kd_c[:, pair]
        v = vd_c[:, pair]
        q_rot, k_rot = (_rope(q, cd, sd), _rope(k, cd, sd)) if rope else (q, k)
        qd_ref[:, blk_a] = jnp.where(upper, swap(q), q_rot).astype(BF16)
        qd_ref[:, blk_b] = jnp.where(upper, q, swap(q_rot)).astype(BF16)
        kd_ref[:, blk_a] = jnp.where(upper, 0.0, k_rot).astype(BF16)
        kd_ref[:, blk_b] = jnp.where(upper, 0.0, swap(k_rot)).astype(BF16)
        vd_ref[:, blk_a] = jnp.where(upper, swap(v), 1.0).astype(BF16)
        vd_ref[:, blk_b] = jnp.where(upper, v, 1.0).astype(BF16)

    uf = proj(_C_UF, F_WIDTH)
    ab_ref[...] = jnp.dot(uf.astype(BF16), cs_ref[...], preferred_element_type=F32).astype(BF16)

    if ctx:
        to_cache(kr_o, proj(_C_XKR, LANE)[:, :MLA_ROPE])


def _inproj_call(l_arr, x, mod5, b_row, g_mix, w1, g_kv, w2, cs64, tables, seq_len, caches, tm):
    t = x.shape[0]
    n_tiles = t // tm
    tiles_per_seq = max(1, seq_len // tm)
    seqs_per_tile, rows = max(1, tm // seq_len), min(tm, seq_len)
    assert tm % rows == 0 and seq_len % rows == 0
    rope = tables is not None
    ctx = caches is not None
    assert not rope or seqs_per_tile == 1
    n1 = _C_XKR + LANE if ctx else _C_XKR
    row = lambda w: pl.BlockSpec((tm, w), lambda i, l: (i, 0))
    modspec = lambda sec: pl.BlockSpec((None, None, None, 1, D_MODEL),
                                       lambda i, l: (l[0], b_row(i, tiles_per_seq), sec, 0, 0))
    in_specs = [
        row(D_MODEL), modspec(0), modspec(1),
        pl.BlockSpec((None, 1, D_MODEL), lambda i, l: (l[0], 0, 0)),
        pl.BlockSpec((None, D_MODEL, n1), lambda i, l: (l[0], 0, 0)),
        pl.BlockSpec((None, 1, MLA_RANK), lambda i, l: (l[0], 0, 0)),
        pl.BlockSpec((None, MLA_RANK, 2 * HW), lambda i, l: (l[0], 0, 0)),
        pl.BlockSpec((F_WIDTH, 2 * F_WIDTH), lambda i, l: (0, 0)),
    ]
    args = [x, mod5, mod5, g_mix, w1, g_kv, w2, cs64]
    if rope:
        in_specs += [pl.BlockSpec((seq_len, LANE), lambda i, l: (0, 0))] * 4
        args += list(tables)
    out_specs = [row(HW)] * 6 + [row(2 * F_WIDTH)]
    out_shape = [jax.ShapeDtypeStruct((t, HW), BF16)] * 6 + [jax.ShapeDtypeStruct((t, 2 * F_WIDTH), BF16)]
    aliases = {}
    if ctx:
        for a in caches:
            aliases[1 + len(args)] = len(out_shape)
            in_specs.append(pl.BlockSpec(memory_space=pl.ANY))
            args.append(a)
            out_specs.append(pl.BlockSpec((seqs_per_tile, None, rows, a.shape[-1]),
                                          lambda i, l: (i // tiles_per_seq, l[0], i % tiles_per_seq, 0)))
            out_shape.append(jax.ShapeDtypeStruct(a.shape, a.dtype))
    return pl.pallas_call(
        functools.partial(_inproj_kernel, rope=rope, ctx=ctx, tiles_per_seq=tiles_per_seq),
        grid_spec=pltpu.PrefetchScalarGridSpec(
            num_scalar_prefetch=1, grid=(n_tiles,), in_specs=in_specs, out_specs=out_specs,
            scratch_shapes=[pltpu.VMEM((tm, D_MODEL), BF16)]),
        out_shape=out_shape,
        input_output_aliases=aliases,
        compiler_params=_cparams(("arbitrary",)),
        name="inproj_ctx" if ctx else "inproj_lat",
    )(l_arr, *args)


def _cacheprep_kernel(ckv_ref, kr_ref, kd_ref, vd_ref, w2_ref, ekr_ref, eup_ref,
                      km_ref, kdo_ref, vm_ref, vdo_ref):
    dot = lambda a, b: jnp.dot(a.astype(BF16), b, preferred_element_type=F32)
    kv2 = dot(ckv_ref[...], w2_ref[...])
    ones_hi = ((lax.broadcasted_iota(jnp.int32, (1, HW), 1) % LANE) >= LANE // 2).astype(F32)
    km_ref[...] = (kv2[:, :HW] + dot(kr_ref[...], ekr_ref[...])).astype(BF16)
    kdo_ref[...] = dot(kd_ref[...], eup_ref[...]).astype(BF16)
    vm_ref[...] = (kv2[:, HW:] + ones_hi).astype(BF16)
    vdo_ref[...] = (dot(vd_ref[...], eup_ref[...]) + (1.0 - ones_hi)).astype(BF16)


def _cacheprep_call(cache_ckv, cache_kr, cache_kd, cache_vd, w2, ekr, eup):
    b, depth, p = cache_ckv.shape[:3]
    cspec = lambda w: pl.BlockSpec((None, None, p, w), lambda l, i: (i, l, 0, 0))
    const = lambda a: pl.BlockSpec(a.shape, lambda l, i: (0,) * a.ndim)
    ospec = pl.BlockSpec((None, None, p, HW), lambda l, i: (l, i, 0, 0))
    oshape = jax.ShapeDtypeStruct((depth, b, p, HW), BF16)
    return pl.pallas_call(
        _cacheprep_kernel,
        grid=(depth, b),
        in_specs=[cspec(MLA_RANK), cspec(MLA_ROPE), cspec(HEADS * 2 * DIFF_DH), cspec(HEADS * DIFF_DV),
                  pl.BlockSpec((None, MLA_RANK, 2 * HW), lambda l, i: (l, 0, 0)),
                  const(ekr), const(eup)],
        out_specs=[ospec] * 4,
        out_shape=[oshape] * 4,
        compiler_params=_cparams(("arbitrary", "arbitrary")),
        name="cacheprep",
    )(cache_ckv, cache_kr, cache_kd, cache_vd, w2, ekr, eup)


def _softmax_pv(q, k_refs, v_refs, blk):
    nt = (((1,), (1,)), ((), ()))
    s = [lax.dot_general(q, k[:, blk], nt, preferred_element_type=F32) for k in k_refs]
    m = functools.reduce(jnp.maximum, [jnp.max(si, axis=-1, keepdims=True) for si in s])
    o = functools.reduce(jnp.add, [jnp.dot(jnp.exp2(si - m).astype(BF16), v[:, blk], preferred_element_type=F32)
                                   for si, v in zip(s, v_refs)])
    return o / pltpu.roll(o, LANE // 2, axis=1)


def _attn_kernel(*refs, n_seg, heads, tq, phased):
    it = iter(refs)
    l_ref = next(it)
    lam_ref, gsub_ref, sub_ref, qm_ref, qd_ref = (next(it) for _ in range(5))
    km = [next(it) for _ in range(n_seg)]
    kd = [next(it) for _ in range(n_seg)]
    vm = [next(it) for _ in range(n_seg)]
    vd = [next(it) for _ in range(n_seg)]
    o_ref = next(it)
    del l_ref

    sub = min(tq, TQ_SUB)
    lane = lax.broadcasted_iota(jnp.int32, (sub, LANE), 1)
    upper = lane >= LANE // 2
    first = (lane // DIFF_DH) % 2 == 0
    nt = (((1,), (1,)), ((), ()))
    problems = [(bi, slice(hd * LANE, (hd + 1) * LANE), slice(r * sub, (r + 1) * sub))
                for bi in range(qm_ref.shape[0]) for hd in range(heads) for r in range(tq // sub)]
    at = lambda refs, bi: [r.at[bi] for r in refs]

    if phased:
        qs, ks, vs, lanes = [], [], [], []
        for bi, blk, rows in problems:
            qd = qd_ref[bi, rows, blk]
            zero = jnp.zeros_like(qd)
            qs += [qm_ref[bi, rows, blk], jnp.where(first, qd, zero), jnp.where(first, zero, qd)]
            ks += [at(km, bi), at(kd, bi), at(kd, bi)]
            vs += [at(vm, bi), at(vd, bi), at(vd, bi)]
            lanes += [blk] * 3
        scores = [[lax.dot_general(q, k[:, blk], nt, preferred_element_type=F32) for k in kk]
                  for q, kk, blk in zip(qs, ks, lanes)]
        maxes = [functools.reduce(jnp.maximum, [jnp.max(si, axis=-1, keepdims=True) for si in s]) for s in scores]
        probs = [[jnp.exp2(si - m).astype(BF16) for si in s] for s, m in zip(scores, maxes)]
        outs = [functools.reduce(jnp.add, [jnp.dot(pi, v[:, blk], preferred_element_type=F32)
                                           for pi, v in zip(p, vv)])
                for p, vv, blk in zip(probs, vs, lanes)]
        outs = [o / pltpu.roll(o, LANE // 2, axis=1) for o in outs]
    else:
        outs = []
        for bi, blk, rows in problems:
            qd = qd_ref[bi, rows, blk]
            zero = jnp.zeros_like(qd)
            outs.append(_softmax_pv(qm_ref[bi, rows, blk], at(km, bi), at(vm, bi), blk))
            outs.append(_softmax_pv(jnp.where(first, qd, zero), at(kd, bi), at(vd, bi), blk))
            outs.append(_softmax_pv(jnp.where(first, zero, qd), at(kd, bi), at(vd, bi), blk))

    for n, (bi, blk, rows) in enumerate(problems):
        o_m, o_1, o_2 = outs[3 * n:3 * n + 3]
        o_d = jnp.where(upper, o_1 - lam_ref[...] * o_2, 0.0)
        ms = jnp.sum(o_d * o_d, axis=-1, keepdims=True) * (1.0 / DIFF_DV)
        o_d = o_d * lax.rsqrt(ms + EPS) * gsub_ref[...] * sub_ref[...]
        o_ref[bi, rows, blk] = jnp.where(upper, o_d, o_m).astype(BF16)


def _attn_call(l_arr, lam, gsub, sub, qm, qd, segs, batch, n_q, tq, heads, bb):
    n_seg = len(segs)
    width = heads * LANE
    q3 = lambda a: a.reshape(batch, n_q, HW)
    layer_vec = pl.BlockSpec((None, 1, LANE), lambda b, h, i, l: (l[0], 0, 0))
    qspec = pl.BlockSpec((bb, tq, width), lambda b, h, i, l: (b, i, h))
    kspecs, kargs, n_keys = [], [], 0
    for which in range(4):
        for seg in segs:
            a = seg[which]
            if a.ndim == 4:
                kspecs.append(pl.BlockSpec((None, bb, a.shape[2], width), lambda b, h, i, l: (l[0], b, 0, h)))
            else:
                kspecs.append(pl.BlockSpec((bb, a.shape[1], width), lambda b, h, i, l: (b, 0, h)))
            kargs.append(a)
            n_keys += a.shape[-2] if which == 0 else 0
    return pl.pallas_call(
        functools.partial(_attn_kernel, n_seg=n_seg, heads=heads, tq=tq, phased=n_keys <= PHASED_MAX_KEYS),
        grid_spec=pltpu.PrefetchScalarGridSpec(
            num_scalar_prefetch=1, grid=(batch // bb, HEADS // heads, n_q // tq),
            in_specs=[layer_vec, layer_vec, layer_vec, qspec, qspec] + kspecs,
            out_specs=qspec),
        out_shape=jax.ShapeDtypeStruct((batch, n_q, HW), BF16),
        compiler_params=_cparams(("arbitrary", "arbitrary", "arbitrary")),
        name="attention_%dseg" % n_seg,
    )(l_arr, lam, gsub, sub, q3(qm), q3(qd), *kargs).reshape(batch * n_q, HW)


def _dft_kernel(c_ref, s_ref, ab_ref, y_ref):
    for bi in range(ab_ref.shape[0]):
        ya = jnp.dot(c_ref[...], ab_ref[bi, :, :F_WIDTH], preferred_element_type=F32)
        yb = jnp.dot(s_ref[...], ab_ref[bi, :, F_WIDTH:], preferred_element_type=F32)
        y_ref[bi] = (ya - yb).astype(BF16)


def _dft_call(cn, sn, ab, batch, n):
    tr = min(n, TM)
    bb = math.gcd(batch, max(1, DFT_ROWS_PER_STEP // n))
    mat = pl.BlockSpec((tr, n), lambda b, r: (r, 0))
    return pl.pallas_call(
        _dft_kernel,
        grid=(batch // bb, n // tr),
        in_specs=[mat, mat, pl.BlockSpec((bb, n, 2 * F_WIDTH), lambda b, r: (b, 0, 0))],
        out_specs=pl.BlockSpec((bb, tr, F_WIDTH), lambda b, r: (b, r, 0)),
        out_shape=jax.ShapeDtypeStruct((batch, n, F_WIDTH), BF16),
        compiler_params=_cparams(("arbitrary", "arbitrary")),
        name="position_dft",
    )(cn, sn, ab.reshape(batch, n, 2 * F_WIDTH)).reshape(batch * n, F_WIDTH)


def _mixffn_kernel(l_ref, o_ref, op_ref, on_ref, y_ref, yp_ref, yn_ref, x_ref, xp_ref, xn_ref,
                   ga_ref, sf_ref, cf_ref, gf_ref, g_ref, gfin_ref, wa_ref, wf_ref, wup_ref, wconv_ref, wdn_ref,
                   xo_ref, lo_ref, ly_ref, hs_ref, hx_ref, act_ref, ys_ref, *, seq_len, final):
    del l_ref
    tm = x_ref.shape[0]
    groups = tm // 8
    tiles_per_seq = max(1, seq_len // tm)
    i = pl.program_id(0)
    has_prev = (i % tiles_per_seq) != 0
    has_next = (i % tiles_per_seq) != tiles_per_seq - 1
    row8 = lax.broadcasted_iota(jnp.int32, (8, FF_CHUNK), 0)
    opens = [s for s in range(1, 8) if (s * groups) % seq_len == 0]
    closes = [s for s in range(0, 7) if ((s + 1) * groups) % seq_len == 0]
    keep_dn = functools.reduce(jnp.logical_and, [row8 != s for s in opens], row8 >= 0)
    keep_up = functools.reduce(jnp.logical_and, [row8 != s for s in closes], row8 >= 0)
    for dst, parts in ((lo_ref, (o_ref, op_ref, on_ref)), (ly_ref, (y_ref, yp_ref, yn_ref))):
        dst[0:tm, :] = parts[0][...]
        dst[tm:tm + HALO, :] = parts[1][...]
        dst[tm + HALO:, :] = parts[2][...]
    attn = jnp.dot(lo_ref[...], wa_ref[...], preferred_element_type=F32)
    attn = attn + jnp.dot(ly_ref[...], wf_ref[...], preferred_element_type=F32)
    x = jnp.concatenate([x_ref[...], xp_ref[...], xn_ref[...]], axis=0) + ga_ref[...] * attn
    xo_ref[...] = x[:tm]
    h = _rms(x) * g_ref[...] * (1.0 + cf_ref[...]) + sf_ref[...]
    mlp_row = lambda j: 8 * ((8 * j) % groups) + (8 * j) // groups
    for j in range(groups):
        for k in range(D_MODEL // LANE):
            hs_ref[k, pl.ds(mlp_row(j), 8, stride=8), :] = h[8 * j:8 * j + 8, k * LANE:(k + 1) * LANE]
    hx_ref[0:tm, :] = jnp.concatenate([hs_ref[k] for k in range(D_MODEL // LANE)], axis=1).astype(BF16)
    zero = jnp.zeros((HALO, D_MODEL), F32)
    hx_ref[tm:tm + HALO, :] = jnp.where(has_prev, h[tm:tm + HALO], zero).astype(BF16)
    hx_ref[tm + HALO:, :] = jnp.where(has_next, h[tm + HALO:], zero).astype(BF16)

    for c in range(N_FF_CHUNKS):
        halves = []
        for cols in (slice(c * FF_CHUNK, (c + 1) * FF_CHUNK),
                     slice(D_FF + c * FF_CHUNK, D_FF + (c + 1) * FF_CHUNK)):
            u = jnp.dot(hx_ref[...], wup_ref[:, cols], preferred_element_type=F32)
            wc = wconv_ref[:, cols]
            mid = u[:tm]
            wrap_dn = jnp.concatenate([u[tm + HALO - 1:tm + HALO], mid[tm - 8:tm - 1]], axis=0)
            wrap_up = jnp.concatenate([mid[1:8], u[tm + HALO:tm + HALO + 1]], axis=0)
            if opens:
                wrap_dn = jnp.where(keep_dn, wrap_dn, 0.0)
            if closes:
                wrap_up = jnp.where(keep_up, wrap_up, 0.0)
            dn = jnp.concatenate([wrap_dn, mid[:tm - 8]], axis=0)
            up = jnp.concatenate([mid[8:], wrap_up], axis=0)
            halves.append(dn * wc[0:1, :] + mid * wc[1:2, :] + up * wc[2:3, :])
        gate, val = halves
        act_ref[:, c * FF_CHUNK:(c + 1) * FF_CHUNK] = (gate * jax.nn.sigmoid(gate) * val).astype(BF16)

    _store_lane_blocks(ys_ref, jnp.dot(act_ref[...], wdn_ref[...], preferred_element_type=F32))
    y = _strided_rows(ys_ref, mlp_row, 8, groups)
    x = xo_ref[...] + gf_ref[...] * y
    if final:
        x = _rms(x) * gfin_ref[...]
    xo_ref[...] = x


def _mixffn_call(l_arr, o, y, x, mod5, b_row, g_ffn, g_final, wo_a, wo_f, wup, wconv, wdn, seq_len, final, tm,
                 shared_cond):
    t = x.shape[0]
    n_tiles = t // tm
    per = tm // HALO
    ext = tm + 2 * HALO

    def with_halos(w):
        return [pl.BlockSpec((tm, w), lambda i, l: (i, 0)),
                pl.BlockSpec((HALO, w), lambda i, l: (jnp.maximum(i * per - 1, 0), 0)),
                pl.BlockSpec((HALO, w), lambda i, l: (jnp.minimum((i + 1) * per, n_tiles * per - 1), 0))]

    assert seq_len % tm == 0 or (tm % seq_len == 0 and shared_cond and seq_len % (tm // 8) == 0)
    tiles_per_seq = max(1, seq_len // tm)
    modspec = lambda sec: pl.BlockSpec((None, None, None, 1, D_MODEL),
                                       lambda i, l: (l[0], b_row(i, tiles_per_seq), sec, 0, 0))
    whole = lambda a: pl.BlockSpec((None,) + a.shape[1:], lambda i, l: (l[0],) + (0,) * (a.ndim - 1))
    return pl.pallas_call(
        functools.partial(_mixffn_kernel, seq_len=seq_len, final=final),
        grid_spec=pltpu.PrefetchScalarGridSpec(
            num_scalar_prefetch=1, grid=(n_tiles,),
            in_specs=with_halos(HW) + with_halos(F_WIDTH) + with_halos(D_MODEL)
            + [modspec(2), modspec(3), modspec(4), modspec(5), whole(g_ffn),
               pl.BlockSpec((1, D_MODEL), lambda i, l: (0, 0)),
               whole(wo_a), whole(wo_f), whole(wup), whole(wconv), whole(wdn)],
            out_specs=pl.BlockSpec((tm, D_MODEL), lambda i, l: (i, 0)),
            scratch_shapes=[pltpu.VMEM((ext, HW), BF16), pltpu.VMEM((ext, F_WIDTH), BF16),
                            pltpu.VMEM((D_MODEL // LANE, tm, LANE), F32), pltpu.VMEM((ext, D_MODEL), BF16),
                            pltpu.VMEM((tm, D_FF), BF16), pltpu.VMEM((D_MODEL // LANE, tm, LANE), F32)]),
        out_shape=jax.ShapeDtypeStruct((t, D_MODEL), F32),
        compiler_params=_cparams(("arbitrary",)),
        name="mix_ffn_final" if final else "mix_ffn",
    )(l_arr, o, o, o, y, y, y, x, x, x, mod5, mod5, mod5, mod5, g_ffn, g_final, wo_a, wo_f, wup, wconv, wdn)


def _dft_mats(n, inner=64):
    j = jnp.arange(n, dtype=jnp.int32)[:, None]
    ang = lambda k: ((j * k[None, :]) % n).astype(F32) * (2.0 * math.pi / n)
    a = ang(jnp.arange(n // inner, dtype=jnp.int32) * inner)[:, :, None]
    b = ang(jnp.arange(inner, dtype=jnp.int32))[:, None, :]
    s = 1.0 / math.sqrt(n)
    ca, sa, cb, sb = jnp.cos(a) * s, jnp.sin(a) * s, jnp.cos(b), jnp.sin(b)
    return ((ca * cb - sa * sb).reshape(n, n).astype(BF16), (sa * cb + ca * sb).reshape(n, n).astype(BF16))


def _channel_dft():
    k = np.arange(F_GROUP_W)
    ang = 2.0 * np.pi * ((k[:, None] * k[None, :]) % F_GROUP_W) / F_GROUP_W
    eye = np.eye(F_GROUPS)
    s = 1.0 / math.sqrt(F_GROUP_W)
    return np.concatenate([np.kron(eye, np.cos(ang) * s), np.kron(eye, np.sin(ang) * s)], axis=1).astype(np.float32)


def kernel(x_prompt, x_sample, c, cache_mla_ckv, cache_mla_krope, cache_diff_k, cache_diff_v, c_ctx, w_ada, b_ada,
           g_mix_norm, g_ffn_norm, w_in, g_kv_norm, w_uk, w_uv, lam_q1, lam_k1, lam_q2, lam_k2, g_diff_subln,
           w_out, w_up, w_conv, w_down, g_final):
    batch_c, seq_c, _ = x_prompt.shape
    batch_l, seq_l, _ = x_sample.shape
    depth = w_in.shape[0]
    assert seq_c % TM == 0 and seq_l % TM == 0 and seq_l % GRID_W == 0 and D_FF % FF_CHUNK == 0
    assert seq_c % TQ_CTX == 0 and seq_l % TQ_LAT == 0 and batch_c % BB_CTX == 0 and seq_l % TM_IN_LAT == 0
    assert seq_l % TM_LAT == 0 and (batch_c * seq_c) % TM_IN_CTX == 0 and HEADS % HEADS_LAT == 0

    w1 = _take(w_in.astype(BF16), _W1_COLS)
    w2 = _take(jnp.concatenate([w_uk, w_uv], axis=-1).astype(BF16), _W2_COLS)
    wo_a = _take(w_out.astype(BF16), _WOUT_ROWS, axis=1)
    wo_f = w_out[:, HEADS * (MLA_DV + DIFF_DV):, :].astype(BF16)
    wup, wconv, wdn = w_up.astype(BF16), w_conv, w_down.astype(BF16)
    vec = lambda a: a.reshape(depth, 1, a.shape[-1])
    g_mix, g_ffn, g_kv = vec(g_mix_norm), vec(g_ffn_norm), vec(g_kv_norm)
    gsub = jnp.concatenate([jnp.ones_like(g_diff_subln), g_diff_subln], axis=-1).reshape(depth, 1, LANE)
    lam_init = np.asarray([0.8 - 0.6 * math.exp(-0.3 * l) for l in range(depth)], np.float32)
    lam_init_rows = jnp.asarray(np.broadcast_to(lam_init[:, None, None], (depth, 1, LANE)))
    sub_rows = jnp.asarray(np.broadcast_to((1.0 - lam_init)[:, None, None], (depth, 1, LANE)).astype(np.float32))
    g_fin = g_final.reshape(1, D_MODEL)
    cs64 = jnp.asarray(_channel_dft()).astype(BF16)
    tables = tuple(jnp.asarray(a) for a in _rope_tables(seq_l))
    dft_c, dft_l = _dft_mats(seq_c), _dft_mats(seq_l)

    n_rows = -(-(batch_l + 1) // 8) * 8
    cond = jnp.zeros((n_rows, D_MODEL), F32).at[:batch_l].set(c).at[batch_l].set(c_ctx)
    mod, lam = _modulation_call(cond, w_ada, b_ada, lam_q1, lam_k1, lam_q2, lam_k2, lam_init_rows)
    mod5 = mod.reshape(depth, n_rows, w_ada.shape[2] // D_MODEL, 1, D_MODEL)
    row_l = lambda i, tiles_per_seq: i // tiles_per_seq
    row_c = lambda i, tiles_per_seq: batch_l

    flat = lambda a: a.reshape(a.shape[0], a.shape[1], a.shape[2], -1)
    past_kv = _cacheprep_call(cache_mla_ckv, cache_mla_krope, flat(cache_diff_k), flat(cache_diff_v), w2,
                              jnp.asarray(_E_KR).astype(BF16), jnp.asarray(_E_UP).astype(BF16))

    x_c = x_prompt.reshape(batch_c * seq_c, D_MODEL)
    x_l = x_sample.reshape(batch_l * seq_l, D_MODEL)
    caches = tuple(jnp.zeros((batch_c, depth, seq_c, w), F32)
                   for w in (MLA_RANK, MLA_ROPE, HEADS * 2 * DIFF_DH, HEADS * DIFF_DV))
    for l in range(depth):
        l_arr = jnp.full((1,), l, jnp.int32)
        final = l == depth - 1
        qm, km, qd, kd, vm, vd, ab, *caches = _inproj_call(
            l_arr, x_c, mod5, row_c, g_mix, w1, g_kv, w2, cs64, None, seq_c, caches, TM_IN_CTX)
        k3 = lambda a: a.reshape(batch_c, seq_c, HW)
        o = _attn_call(l_arr, lam, gsub, sub_rows, qm, qd, [(k3(km), k3(kd), k3(vm), k3(vd))],
                       batch_c, seq_c, TQ_CTX, HEADS, BB_CTX)
        y = _dft_call(*dft_c, ab, batch_c, seq_c)
        x_c = _mixffn_call(l_arr, o, y, x_c, mod5, row_c, g_ffn, g_fin, wo_a, wo_f, wup, wconv, wdn, seq_c, final,
                           TM_CTX, True)
        qm, km, qd, kd, vm, vd, ab = _inproj_call(
            l_arr, x_l, mod5, row_l, g_mix, w1, g_kv, w2, cs64, tables, seq_l, None, TM_IN_LAT)
        k3 = lambda a: a.reshape(batch_l, seq_l, HW)
        o = _attn_call(l_arr, lam, gsub, sub_rows, qm, qd, [(k3(km), k3(kd), k3(vm), k3(vd)), past_kv],
                       batch_l, seq_l, TQ_LAT, HEADS_LAT, 1)
        y = _dft_call(*dft_l, ab, batch_l, seq_l)
        x_l = _mixffn_call(l_arr, o, y, x_l, mod5, row_l, g_ffn, g_fin, wo_a, wo_f, wup, wconv, wdn, seq_l, final,
                           TM_LAT, False)

    new_ckv, new_kr, new_kd, new_vd = caches
    return (x_c.reshape(batch_c, seq_c, D_MODEL),
            x_l.reshape(batch_l, seq_l, D_MODEL),
            new_ckv, new_kr,
            new_kd.reshape(batch_c, depth, seq_c, HEADS, 2 * DIFF_DH),
            new_vd.reshape(batch_c, depth, seq_c, HEADS, DIFF_DV))
```

```python
import functools
import math

import numpy as np
import jax
import jax.numpy as jnp
from jax import lax
from jax.experimental import pallas as pl
from jax.experimental.pallas import tpu as pltpu

F32 = jnp.float32
BF16 = jnp.bfloat16

D_MODEL = 1024
GRID_W = 64
HEADS = 6
MLA_NOPE, MLA_ROPE, MLA_DV, MLA_RANK = 64, 32, 64, 128
DIFF_DH, DIFF_DV = 32, 64
F_GROUPS, F_GROUP_W = 4, 64
F_WIDTH = F_GROUPS * F_GROUP_W
D_FF = 2816
ROPE_BASE = 10000.0
EPS = 1e-6
MLA_SCALE = (MLA_NOPE + MLA_ROPE) ** -0.5
DIFF_SCALE = DIFF_DH ** -0.5
LOG2E = math.log2(math.e)

_QM0 = 0
_CKV0 = HEADS * (MLA_NOPE + MLA_ROPE)
_KR0 = _CKV0 + MLA_RANK
_QD0 = _KR0 + MLA_ROPE
_KD0 = _QD0 + HEADS * 2 * DIFF_DH
_VD0 = _KD0 + HEADS * 2 * DIFF_DH
_UF0 = _VD0 + HEADS * DIFF_DV

LANE = 128
HW = HEADS * LANE
TM = 256
TM_CTX = 512
TM_LAT = 512
TM_IN_LAT = 1024
TM_IN_CTX = 512
TQ_CTX = 256
TQ_LAT = 1024
HEADS_LAT = 1
TQ_SUB = 512
BB_CTX = 4
DFT_ROWS_PER_STEP = 8192
PHASED_SCORE_BYTES = 32 << 20
FF_CHUNK = 256
N_FF_CHUNKS = D_FF // FF_CHUNK
HALO = 8
VMEM_LIMIT = 56 * 1024 * 1024


def _cparams(sem):
    return pltpu.CompilerParams(dimension_semantics=sem, vmem_limit_bytes=VMEM_LIMIT)


def _w1_columns():
    z = lambda n: [-1] * n
    r = lambda a, n: list(range(a, a + n))
    cols = []
    cols += r(_CKV0, MLA_RANK)
    cols += z(MLA_NOPE) + r(_KR0, MLA_ROPE) + z(LANE - MLA_NOPE - MLA_ROPE)
    for h in range(HEADS):
        b = _QM0 + h * (MLA_NOPE + MLA_ROPE)
        cols += r(b, MLA_NOPE) + r(b + MLA_NOPE, MLA_ROPE) + r(b + MLA_NOPE, MLA_ROPE)
    cols += r(_QD0, _UF0 + F_WIDTH - _QD0)
    cols += r(_KR0, MLA_ROPE) + z(LANE - MLA_ROPE)
    return np.asarray(cols, np.int32)


_W1_COLS = _w1_columns()
DIFF_W = HEADS * 2 * DIFF_DH
assert DIFF_W == HEADS * DIFF_DV and 2 * DIFF_DH == LANE // 2
_C_CKV, _C_KR, _C_QM = 0, LANE, 2 * LANE
_C_QD = _C_QM + HW
_C_KD = _C_QD + DIFF_W
_C_VD = _C_KD + DIFF_W
_C_UF = _C_VD + DIFF_W
_C_XKR = _C_UF + F_WIDTH


def _w2_columns():
    cols = []
    for base in (0, HEADS * MLA_NOPE):
        for h in range(HEADS):
            cols += list(range(base + h * 64, base + (h + 1) * 64)) + [-1] * 64
    return np.asarray(cols, np.int32)


_W2_COLS = _w2_columns()


def _wout_rows():
    rows = []
    for h in range(HEADS):
        rows += list(range(h * MLA_DV, (h + 1) * MLA_DV))
        rows += list(range(HEADS * MLA_DV + h * DIFF_DV, HEADS * MLA_DV + (h + 1) * DIFF_DV))
    return np.asarray(rows, np.int32)


_WOUT_ROWS = _wout_rows()


def _take(w, idx, axis=-1):
    axis = axis % w.ndim
    pieces, i, n = [], 0, len(idx)
    while i < n:
        j = i + 1
        if idx[i] < 0:
            while j < n and idx[j] < 0:
                j += 1
            shape = w.shape[:axis] + (j - i,) + w.shape[axis + 1:]
            pieces.append(jnp.zeros(shape, w.dtype))
        else:
            while j < n and idx[j] == idx[j - 1] + 1:
                j += 1
            pieces.append(lax.slice_in_dim(w, int(idx[i]), int(idx[i]) + j - i, axis=axis))
        i = j
    return jnp.concatenate(pieces, axis=axis)


def _selection(src_of_col, n_src):
    e = np.zeros((n_src, len(src_of_col)), np.float32)
    for c, s in enumerate(src_of_col):
        if s >= 0:
            e[s, c] = 1.0
    return e


def _cache_selections():
    kr, up = [], []
    for h in range(HEADS):
        kr += [-1] * (MLA_NOPE + MLA_ROPE) + list(range(MLA_ROPE))
        up += [-1] * (LANE // 2) + list(range(h * LANE // 2, (h + 1) * LANE // 2))
    return _selection(kr, MLA_ROPE), _selection(up, DIFF_W)


_E_KR, _E_UP = _cache_selections()


def _rope_tables(n_lat):
    t = np.arange(n_lat)
    row = (t // GRID_W).astype(np.float64)
    col = (t % GRID_W).astype(np.float64)
    quarter = MLA_ROPE // 4
    inv = ROPE_BASE ** (-np.arange(quarter, dtype=np.float64) / quarter)
    inv = inv.astype(np.float32).astype(np.float64)
    ang = np.concatenate([row[:, None] * inv, row[:, None] * inv, col[:, None] * inv, col[:, None] * inv], axis=1)
    ang = ang.astype(np.float32)
    cos32 = np.cos(ang).astype(np.float32)
    sign = np.concatenate([-np.ones(quarter), np.ones(quarter)] * 2).astype(np.float32)
    sin32 = np.sin(ang).astype(np.float32) * sign
    one, zero = np.ones((n_lat, 32), np.float32), np.zeros((n_lat, 32), np.float32)
    cm = np.concatenate([one, one, cos32, one], axis=1)
    sm = np.concatenate([zero, zero, sin32, zero], axis=1)
    cd = np.concatenate([cos32] * 4, axis=1)
    sd = np.concatenate([sin32] * 4, axis=1)
    return cm, sm, cd, sd


def _mod_kernel(cond_ref, w_ref, b_ref, q1_ref, k1_ref, q2_ref, k2_ref, li_ref, mod_ref, lam_ref):
    a = cond_ref[...]
    a = a * jax.nn.sigmoid(a)
    acc = jnp.dot(a.astype(BF16), w_ref[...].astype(BF16), preferred_element_type=F32)
    mod_ref[...] = acc + b_ref[...]
    d1 = jnp.sum(q1_ref[...] * k1_ref[...], axis=-1, keepdims=True)
    d2 = jnp.sum(q2_ref[...] * k2_ref[...], axis=-1, keepdims=True)
    lam_ref[...] = jnp.exp(d1) - jnp.exp(d2) + li_ref[...]


def _modulation_call(cond, w_ada, b_ada, lam_q1, lam_k1, lam_q2, lam_k2, lam_init_rows):
    depth = w_ada.shape[0]
    n_sec = w_ada.shape[2] // D_MODEL
    rows = cond.shape[0]
    vec = lambda a: a.reshape(depth, 1, a.shape[-1])
    lam_spec = pl.BlockSpec((None, 1, DIFF_DH), lambda l, s: (l, 0, 0))
    return pl.pallas_call(
        _mod_kernel,
        grid=(depth, n_sec),
        in_specs=[
            pl.BlockSpec((rows, D_MODEL), lambda l, s: (0, 0)),
            pl.BlockSpec((None, D_MODEL, D_MODEL), lambda l, s: (l, 0, s)),
            pl.BlockSpec((None, 1, D_MODEL), lambda l, s: (l, 0, s)),
            lam_spec, lam_spec, lam_spec, lam_spec,
            pl.BlockSpec((None, 1, LANE), lambda l, s: (l, 0, 0)),
        ],
        out_specs=[
            pl.BlockSpec((None, rows, D_MODEL), lambda l, s: (l, 0, s)),
            pl.BlockSpec((None, 1, LANE), lambda l, s: (l, 0, 0)),
        ],
        out_shape=[
            jax.ShapeDtypeStruct((depth, rows, n_sec * D_MODEL), F32),
            jax.ShapeDtypeStruct((depth, 1, LANE), F32),
        ],
        compiler_params=_cparams(("arbitrary", "arbitrary")),
        name="modulation",
    )(cond, w_ada, vec(b_ada), vec(lam_q1), vec(lam_k1), vec(lam_q2), vec(lam_k2), lam_init_rows)


def _rms(x):
    return x * lax.rsqrt(jnp.mean(x * x, axis=-1, keepdims=True) + EPS)


def _swap_halves(x):
    lane = lax.broadcasted_iota(jnp.int32, x.shape, 1)
    first = (lane & 8) == 0
    return jnp.where(first, pltpu.roll(x, LANE - 8, axis=1), pltpu.roll(x, 8, axis=1))


def _rope(x, c, s):
    return x * c + _swap_halves(x) * s


def _store_lane_blocks(ref, x):
    for k in range(ref.shape[0]):
        ref[k] = x[:, k * LANE:(k + 1) * LANE]


def _strided_rows(ref, start_of_group, stride, groups):
    return jnp.concatenate(
        [jnp.concatenate([ref[k, pl.ds(start_of_group(j), 8, stride=stride), :] for k in range(ref.shape[0])], axis=1)
         for j in range(groups)], axis=0)


def _inproj_kernel(*refs, rope, ctx, tiles_per_seq):
    it = iter(refs)
    l_ref = next(it)
    x_ref, shift_ref, scale_ref, g_ref, w1_ref, gkv_ref, w2_ref, cs_ref = (next(it) for _ in range(8))
    if rope:
        cm_ref, sm_ref, cd_ref, sd_ref = (next(it) for _ in range(4))
    if ctx:
        for _ in range(4):
            next(it)
    qm_ref, km_ref, qd_ref, kd_ref, vm_ref, vd_ref, ab_ref = (next(it) for _ in range(7))
    if ctx:
        ckv_o, kr_o, kdc_o, vdc_o = (next(it) for _ in range(4))
    h_scr = next(it)
    del l_ref
    tm = x_ref.shape[0]
    upper = lax.broadcasted_iota(jnp.int32, (tm, LANE), 1) >= LANE // 2
    ones_hi = upper.astype(F32)

    x = x_ref[...]
    h = _rms(x) * g_ref[...] * (1.0 + scale_ref[...]) + shift_ref[...]
    h_scr[...] = h.astype(BF16)

    if rope:
        r0 = pl.multiple_of((pl.program_id(0) % tiles_per_seq) * tm, tm)
        cm, sm = cm_ref[pl.ds(r0, tm), :], sm_ref[pl.ds(r0, tm), :]
        cd, sd = cd_ref[pl.ds(r0, tm), :], sd_ref[pl.ds(r0, tm), :]

    def proj(c0, width):
        return jnp.dot(h_scr[...], w1_ref[:, c0:c0 + width], preferred_element_type=F32)

    acc = proj(_C_CKV, 2 * LANE)
    ckvn = _rms(acc[:, :LANE]) * gkv_ref[...]
    kr = acc[:, LANE:]
    def to_cache(ref, val):
        rows = ref.shape[1]
        for s in range(ref.shape[0]):
            ref[s] = val[s * rows:(s + 1) * rows]

    if ctx:
        to_cache(ckv_o, ckvn)
    if rope:
        kr = _rope(kr, cm, sm)
    kv2 = jnp.dot(ckvn.astype(BF16), w2_ref[...], preferred_element_type=F32)
    for hd in range(HEADS):
        blk = slice(hd * LANE, (hd + 1) * LANE)
        km_ref[:, blk] = (kv2[:, blk] + kr).astype(BF16)
        vm_ref[:, blk] = (kv2[:, HW + hd * LANE:HW + (hd + 1) * LANE] + ones_hi).astype(BF16)

    for j in range(HW // (2 * LANE)):
        acc = proj(_C_QM + j * 2 * LANE, 2 * LANE)
        for k in range(2):
            blk = acc[:, k * LANE:(k + 1) * LANE]
            if rope:
                blk = _rope(blk, cm, sm)
            col = (2 * j + k) * LANE
            qm_ref[:, col:col + LANE] = (blk * (MLA_SCALE * LOG2E)).astype(BF16)

    qd_c, kd_c, vd_c = proj(_C_QD, DIFF_W), proj(_C_KD, DIFF_W), proj(_C_VD, DIFF_W)
    if ctx:
        to_cache(kdc_o, kd_c)
        to_cache(vdc_o, vd_c)
    swap = lambda a: pltpu.roll(a, LANE // 2, axis=1)
    for j in range(DIFF_W // LANE):
        pair = slice(j * LANE, (j + 1) * LANE)
        blk_a, blk_b = slice(2 * j * LANE, (2 * j + 1) * LANE), slice((2 * j + 1) * LANE, (2 * j + 2) * LANE)
        q = qd_c[:, pair] * (DIFF_SCALE * LOG2E)
        k = kd_c[:, pair]
        v = vd_c[:, pair]
        q_rot, k_rot = (_rope(q, cd, sd), _rope(k, cd, sd)) if rope else (q, k)
        qd_ref[:, blk_a] = jnp.where(upper, swap(q), q_rot).astype(BF16)
        qd_ref[:, blk_b] = jnp.where(upper, q, swap(q_rot)).astype(BF16)
        kd_ref[:, blk_a] = jnp.where(upper, 0.0, k_rot).astype(BF16)
        kd_ref[:, blk_b] = jnp.where(upper, 0.0, swap(k_rot)).astype(BF16)
        vd_ref[:, blk_a] = jnp.where(upper, swap(v), 1.0).astype(BF16)
        vd_ref[:, blk_b] = jnp.where(upper, v, 1.0).astype(BF16)

    uf = proj(_C_UF, F_WIDTH)
    ab_ref[...] = jnp.dot(uf.astype(BF16), cs_ref[...], preferred_element_type=F32).astype(BF16)

    if ctx:
        to_cache(kr_o, proj(_C_XKR, LANE)[:, :MLA_ROPE])


def _inproj_call(l_arr, x, mod5, b_row, g_mix, w1, g_kv, w2, cs64, tables, seq_len, caches, tm):
    t = x.shape[0]
    n_tiles = t // tm
    tiles_per_seq = max(1, seq_len // tm)
    seqs_per_tile, rows = max(1, tm // seq_len), min(tm, seq_len)
    assert tm % rows == 0 and seq_len % rows == 0
    rope = tables is not None
    ctx = caches is not None
    assert not rope or seqs_per_tile == 1
    n1 = _C_XKR + LANE if ctx else _C_XKR
    row = lambda w: pl.BlockSpec((tm, w), lambda i, l: (i, 0))
    modspec = lambda sec: pl.BlockSpec((None, None, None, 1, D_MODEL),
                                       lambda i, l: (l[0], b_row(i, tiles_per_seq), sec, 0, 0))
    in_specs = [
        row(D_MODEL), modspec(0), modspec(1),
        pl.BlockSpec((None, 1, D_MODEL), lambda i, l: (l[0], 0, 0)),
        pl.BlockSpec((None, D_MODEL, n1), lambda i, l: (l[0], 0, 0)),
        pl.BlockSpec((None, 1, MLA_RANK), lambda i, l: (l[0], 0, 0)),
        pl.BlockSpec((None, MLA_RANK, 2 * HW), lambda i, l: (l[0], 0, 0)),
        pl.BlockSpec((F_WIDTH, 2 * F_WIDTH), lambda i, l: (0, 0)),
    ]
    args = [x, mod5, mod5, g_mix, w1, g_kv, w2, cs64]
    if rope:
        in_specs += [pl.BlockSpec((seq_len, LANE), lambda i, l: (0, 0))] * 4
        args += list(tables)
    out_specs = [row(HW)] * 6 + [row(2 * F_WIDTH)]
    out_shape = [jax.ShapeDtypeStruct((t, HW), BF16)] * 6 + [jax.ShapeDtypeStruct((t, 2 * F_WIDTH), BF16)]
    aliases = {}
    if ctx:
        for a in caches:
            aliases[1 + len(args)] = len(out_shape)
            in_specs.append(pl.BlockSpec(memory_space=pl.ANY))
            args.append(a)
            out_specs.append(pl.BlockSpec((seqs_per_tile, None, rows, a.shape[-1]),
                                          lambda i, l: (i // tiles_per_seq, l[0], i % tiles_per_seq, 0)))
            out_shape.append(jax.ShapeDtypeStruct(a.shape, a.dtype))
    return pl.pallas_call(
        functools.partial(_inproj_kernel, rope=rope, ctx=ctx, tiles_per_seq=tiles_per_seq),
        grid_spec=pltpu.PrefetchScalarGridSpec(
            num_scalar_prefetch=1, grid=(n_tiles,), in_specs=in_specs, out_specs=out_specs,
            scratch_shapes=[pltpu.VMEM((tm, D_MODEL), BF16)]),
        out_shape=out_shape,
        input_output_aliases=aliases,
        compiler_params=_cparams(("arbitrary",)),
        name="inproj_ctx" if ctx else "inproj_lat",
    )(l_arr, *args)


def _cacheprep_kernel(ckv_ref, kr_ref, kd_ref, vd_ref, w2_ref, ekr_ref, eup_ref,
                      km_ref, kdo_ref, vm_ref, vdo_ref):
    dot = lambda a, b: jnp.dot(a.astype(BF16), b, preferred_element_type=F32)
    kv2 = dot(ckv_ref[...], w2_ref[...])
    ones_hi = ((lax.broadcasted_iota(jnp.int32, (1, HW), 1) % LANE) >= LANE // 2).astype(F32)
    km_ref[...] = (kv2[:, :HW] + dot(kr_ref[...], ekr_ref[...])).astype(BF16)
    kdo_ref[...] = dot(kd_ref[...], eup_ref[...]).astype(BF16)
    vm_ref[...] = (kv2[:, HW:] + ones_hi).astype(BF16)
    vdo_ref[...] = (dot(vd_ref[...], eup_ref[...]) + (1.0 - ones_hi)).astype(BF16)


def _cacheprep_call(cache_ckv, cache_kr, cache_kd, cache_vd, w2, ekr, eup):
    b, depth, p = cache_ckv.shape[:3]
    cspec = lambda w: pl.BlockSpec((None, None, p, w), lambda l, i: (i, l, 0, 0))
    const = lambda a: pl.BlockSpec(a.shape, lambda l, i: (0,) * a.ndim)
    ospec = pl.BlockSpec((None, None, p, HW), lambda l, i: (l, i, 0, 0))
    oshape = jax.ShapeDtypeStruct((depth, b, p, HW), BF16)
    return pl.pallas_call(
        _cacheprep_kernel,
        grid=(depth, b),
        in_specs=[cspec(MLA_RANK), cspec(MLA_ROPE), cspec(HEADS * 2 * DIFF_DH), cspec(HEADS * DIFF_DV),
                  pl.BlockSpec((None, MLA_RANK, 2 * HW), lambda l, i: (l, 0, 0)),
                  const(ekr), const(eup)],
        out_specs=[ospec] * 4,
        out_shape=[oshape] * 4,
        compiler_params=_cparams(("arbitrary", "arbitrary")),
        name="cacheprep",
    )(cache_ckv, cache_kr, cache_kd, cache_vd, w2, ekr, eup)


def _softmax_pv(q, k_refs, v_refs, blk):
    nt = (((1,), (1,)), ((), ()))
    s = [lax.dot_general(q, k[:, blk], nt, preferred_element_type=F32) for k in k_refs]
    m = functools.reduce(jnp.maximum, [jnp.max(si, axis=-1, keepdims=True) for si in s])
    o = functools.reduce(jnp.add, [jnp.dot(jnp.exp2(si - m).astype(BF16), v[:, blk], preferred_element_type=F32)
                                   for si, v in zip(s, v_refs)])
    return o / pltpu.roll(o, LANE // 2, axis=1)


def _attn_kernel(*refs, n_seg, heads, tq, phased):
    it = iter(refs)
    l_ref = next(it)
    lam_ref, gsub_ref, sub_ref, qm_ref, qd_ref = (next(it) for _ in range(5))
    km = [next(it) for _ in range(n_seg)]
    kd = [next(it) for _ in range(n_seg)]
    vm = [next(it) for _ in range(n_seg)]
    vd = [next(it) for _ in range(n_seg)]
    o_ref = next(it)
    del l_ref

    sub = min(tq, TQ_SUB)
    lane = lax.broadcasted_iota(jnp.int32, (sub, LANE), 1)
    upper = lane >= LANE // 2
    first = (lane // DIFF_DH) % 2 == 0
    nt = (((1,), (1,)), ((), ()))
    problems = [(bi, slice(hd * LANE, (hd + 1) * LANE), slice(r * sub, (r + 1) * sub))
                for bi in range(qm_ref.shape[0]) for hd in range(heads) for r in range(tq // sub)]
    at = lambda refs, bi: [r.at[bi] for r in refs]

    if phased:
        qs, ks, vs, lanes = [], [], [], []
        for bi, blk, rows in problems:
            qd = qd_ref[bi, rows, blk]
            zero = jnp.zeros_like(qd)
            qs += [qm_ref[bi, rows, blk], jnp.where(first, qd, zero), jnp.where(first, zero, qd)]
            ks += [at(km, bi), at(kd, bi), at(kd, bi)]
            vs += [at(vm, bi), at(vd, bi), at(vd, bi)]
            lanes += [blk] * 3
        scores = [[lax.dot_general(q, k[:, blk], nt, preferred_element_type=F32) for k in kk]
                  for q, kk, blk in zip(qs, ks, lanes)]
        maxes = [functools.reduce(jnp.maximum, [jnp.max(si, axis=-1, keepdims=True) for si in s]) for s in scores]
        probs = [[jnp.exp2(si - m).astype(BF16) for si in s] for s, m in zip(scores, maxes)]
        outs = [functools.reduce(jnp.add, [jnp.dot(pi, v[:, blk], preferred_element_type=F32)
                                           for pi, v in zip(p, vv)])
                for p, vv, blk in zip(probs, vs, lanes)]
        outs = [o / pltpu.roll(o, LANE // 2, axis=1) for o in outs]
    else:
        outs = []
        for bi, blk, rows in problems:
            qd = qd_ref[bi, rows, blk]
            zero = jnp.zeros_like(qd)
            outs.append(_softmax_pv(qm_ref[bi, rows, blk], at(km, bi), at(vm, bi), blk))
            outs.append(_softmax_pv(jnp.where(first, qd, zero), at(kd, bi), at(vd, bi), blk))
            outs.append(_softmax_pv(jnp.where(first, zero, qd), at(kd, bi), at(vd, bi), blk))

    for n, (bi, blk, rows) in enumerate(problems):
        o_m, o_1, o_2 = outs[3 * n:3 * n + 3]
        o_d = jnp.where(upper, o_1 - lam_ref[...] * o_2, 0.0)
        ms = jnp.sum(o_d * o_d, axis=-1, keepdims=True) * (1.0 / DIFF_DV)
        o_d = o_d * lax.rsqrt(ms + EPS) * gsub_ref[...] * sub_ref[...]
        o_ref[bi, rows, blk] = jnp.where(upper, o_d, o_m).astype(BF16)


def _attn_call(l_arr, lam, gsub, sub, qm, qd, segs, batch, n_q, tq, heads, bb):
    n_seg = len(segs)
    width = heads * LANE
    q3 = lambda a: a.reshape(batch, n_q, HW)
    layer_vec = pl.BlockSpec((None, 1, LANE), lambda b, h, i, l: (l[0], 0, 0))
    qspec = pl.BlockSpec((bb, tq, width), lambda b, h, i, l: (b, i, h))
    kspecs, kargs, n_keys = [], [], 0
    for which in range(4):
        for seg in segs:
            a = seg[which]
            if a.ndim == 4:
                kspecs.append(pl.BlockSpec((None, bb, a.shape[2], width), lambda b, h, i, l: (l[0], b, 0, h)))
            else:
                kspecs.append(pl.BlockSpec((bb, a.shape[1], width), lambda b, h, i, l: (b, 0, h)))
            kargs.append(a)
            n_keys += a.shape[-2] if which == 0 else 0
    return pl.pallas_call(
        functools.partial(_attn_kernel, n_seg=n_seg, heads=heads, tq=tq,
                          phased=3 * bb * heads * tq * n_keys * 4 <= PHASED_SCORE_BYTES),
        grid_spec=pltpu.PrefetchScalarGridSpec(
            num_scalar_prefetch=1, grid=(batch // bb, HEADS // heads, n_q // tq),
            in_specs=[layer_vec, layer_vec, layer_vec, qspec, qspec] + kspecs,
            out_specs=qspec),
        out_shape=jax.ShapeDtypeStruct((batch, n_q, HW), BF16),
        compiler_params=_cparams(("arbitrary", "arbitrary", "arbitrary")),
        name="attention_%dseg" % n_seg,
    )(l_arr, lam, gsub, sub, q3(qm), q3(qd), *kargs).reshape(batch * n_q, HW)


def _dft_kernel(c_ref, s_ref, ab_ref, y_ref):
    for bi in range(ab_ref.shape[0]):
        ya = jnp.dot(c_ref[...], ab_ref[bi, :, :F_WIDTH], preferred_element_type=F32)
        yb = jnp.dot(s_ref[...], ab_ref[bi, :, F_WIDTH:], preferred_element_type=F32)
        y_ref[bi] = (ya - yb).astype(BF16)


def _dft_call(cn, sn, ab, batch, n):
    tr = min(n, TM)
    bb = math.gcd(batch, max(1, DFT_ROWS_PER_STEP // n))
    mat = pl.BlockSpec((tr, n), lambda b, r: (r, 0))
    return pl.pallas_call(
        _dft_kernel,
        grid=(batch // bb, n // tr),
        in_specs=[mat, mat, pl.BlockSpec((bb, n, 2 * F_WIDTH), lambda b, r: (b, 0, 0))],
        out_specs=pl.BlockSpec((bb, tr, F_WIDTH), lambda b, r: (b, r, 0)),
        out_shape=jax.ShapeDtypeStruct((batch, n, F_WIDTH), BF16),
        compiler_params=_cparams(("arbitrary", "arbitrary")),
        name="position_dft",
    )(cn, sn, ab.reshape(batch, n, 2 * F_WIDTH)).reshape(batch * n, F_WIDTH)


def _mixffn_kernel(l_ref, o_ref, op_ref, on_ref, y_ref, yp_ref, yn_ref, x_ref, xp_ref, xn_ref,
                   ga_ref, sf_ref, cf_ref, gf_ref, g_ref, gfin_ref, wa_ref, wf_ref, wup_ref, wconv_ref, wdn_ref,
                   xo_ref, lo_ref, ly_ref, hs_ref, hx_ref, act_ref, ys_ref, *, seq_len, final):
    del l_ref
    tm = x_ref.shape[0]
    groups = tm // 8
    tiles_per_seq = max(1, seq_len // tm)
    i = pl.program_id(0)
    has_prev = (i % tiles_per_seq) != 0
    has_next = (i % tiles_per_seq) != tiles_per_seq - 1
    row8 = lax.broadcasted_iota(jnp.int32, (8, FF_CHUNK), 0)
    opens = [s for s in range(1, 8) if (s * groups) % seq_len == 0]
    closes = [s for s in range(0, 7) if ((s + 1) * groups) % seq_len == 0]
    keep_dn = functools.reduce(jnp.logical_and, [row8 != s for s in opens], row8 >= 0)
    keep_up = functools.reduce(jnp.logical_and, [row8 != s for s in closes], row8 >= 0)
    for dst, parts in ((lo_ref, (o_ref, op_ref, on_ref)), (ly_ref, (y_ref, yp_ref, yn_ref))):
        dst[0:tm, :] = parts[0][...]
        dst[tm:tm + HALO, :] = parts[1][...]
        dst[tm + HALO:, :] = parts[2][...]
    attn = jnp.dot(lo_ref[...], wa_ref[...], preferred_element_type=F32)
    attn = attn + jnp.dot(ly_ref[...], wf_ref[...], preferred_element_type=F32)
    x = jnp.concatenate([x_ref[...], xp_ref[...], xn_ref[...]], axis=0) + ga_ref[...] * attn
    xo_ref[...] = x[:tm]
    h = _rms(x) * g_ref[...] * (1.0 + cf_ref[...]) + sf_ref[...]
    mlp_row = lambda j: 8 * ((8 * j) % groups) + (8 * j) // groups
    for j in range(groups):
        for k in range(D_MODEL // LANE):
            hs_ref[k, pl.ds(mlp_row(j), 8, stride=8), :] = h[8 * j:8 * j + 8, k * LANE:(k + 1) * LANE]
    hx_ref[0:tm, :] = jnp.concatenate([hs_ref[k] for k in range(D_MODEL // LANE)], axis=1).astype(BF16)
    zero = jnp.zeros((HALO, D_MODEL), F32)
    hx_ref[tm:tm + HALO, :] = jnp.where(has_prev, h[tm:tm + HALO], zero).astype(BF16)
    hx_ref[tm + HALO:, :] = jnp.where(has_next, h[tm + HALO:], zero).astype(BF16)

    for c in range(N_FF_CHUNKS):
        halves = []
        for cols in (slice(c * FF_CHUNK, (c + 1) * FF_CHUNK),
                     slice(D_FF + c * FF_CHUNK, D_FF + (c + 1) * FF_CHUNK)):
            u = jnp.dot(hx_ref[...], wup_ref[:, cols], preferred_element_type=F32)
            wc = wconv_ref[:, cols]
            mid = u[:tm]
            wrap_dn = jnp.concatenate([u[tm + HALO - 1:tm + HALO], mid[tm - 8:tm - 1]], axis=0)
            wrap_up = jnp.concatenate([mid[1:8], u[tm + HALO:tm + HALO + 1]], axis=0)
            if opens:
                wrap_dn = jnp.where(keep_dn, wrap_dn, 0.0)
            if closes:
                wrap_up = jnp.where(keep_up, wrap_up, 0.0)
            dn = jnp.concatenate([wrap_dn, mid[:tm - 8]], axis=0)
            up = jnp.concatenate([mid[8:], wrap_up], axis=0)
            halves.append(dn * wc[0:1, :] + mid * wc[1:2, :] + up * wc[2:3, :])
        gate, val = halves
        act_ref[:, c * FF_CHUNK:(c + 1) * FF_CHUNK] = (gate * jax.nn.sigmoid(gate) * val).astype(BF16)

    _store_lane_blocks(ys_ref, jnp.dot(act_ref[...], wdn_ref[...], preferred_element_type=F32))
    y = _strided_rows(ys_ref, mlp_row, 8, groups)
    x = xo_ref[...] + gf_ref[...] * y
    if final:
        x = _rms(x) * gfin_ref[...]
    xo_ref[...] = x


def _mixffn_call(l_arr, o, y, x, mod5, b_row, g_ffn, g_final, wo_a, wo_f, wup, wconv, wdn, seq_len, final, tm,
                 shared_cond):
    t = x.shape[0]
    n_tiles = t // tm
    per = tm // HALO
    ext = tm + 2 * HALO

    def with_halos(w):
        return [pl.BlockSpec((tm, w), lambda i, l: (i, 0)),
                pl.BlockSpec((HALO, w), lambda i, l: (jnp.maximum(i * per - 1, 0), 0)),
                pl.BlockSpec((HALO, w), lambda i, l: (jnp.minimum((i + 1) * per, n_tiles * per - 1), 0))]

    assert seq_len % tm == 0 or (tm % seq_len == 0 and shared_cond and seq_len % (tm // 8) == 0)
    tiles_per_seq = max(1, seq_len // tm)
    modspec = lambda sec: pl.BlockSpec((None, None, None, 1, D_MODEL),
                                       lambda i, l: (l[0], b_row(i, tiles_per_seq), sec, 0, 0))
    whole = lambda a: pl.BlockSpec((None,) + a.shape[1:], lambda i, l: (l[0],) + (0,) * (a.ndim - 1))
    return pl.pallas_call(
        functools.partial(_mixffn_kernel, seq_len=seq_len, final=final),
        grid_spec=pltpu.PrefetchScalarGridSpec(
            num_scalar_prefetch=1, grid=(n_tiles,),
            in_specs=with_halos(HW) + with_halos(F_WIDTH) + with_halos(D_MODEL)
            + [modspec(2), modspec(3), modspec(4), modspec(5), whole(g_ffn),
               pl.BlockSpec((1, D_MODEL), lambda i, l: (0, 0)),
               whole(wo_a), whole(wo_f), whole(wup), whole(wconv), whole(wdn)],
            out_specs=pl.BlockSpec((tm, D_MODEL), lambda i, l: (i, 0)),
            scratch_shapes=[pltpu.VMEM((ext, HW), BF16), pltpu.VMEM((ext, F_WIDTH), BF16),
                            pltpu.VMEM((D_MODEL // LANE, tm, LANE), F32), pltpu.VMEM((ext, D_MODEL), BF16),
                            pltpu.VMEM((tm, D_FF), BF16), pltpu.VMEM((D_MODEL // LANE, tm, LANE), F32)]),
        out_shape=jax.ShapeDtypeStruct((t, D_MODEL), F32),
        compiler_params=_cparams(("arbitrary",)),
        name="mix_ffn_final" if final else "mix_ffn",
    )(l_arr, o, o, o, y, y, y, x, x, x, mod5, mod5, mod5, mod5, g_ffn, g_final, wo_a, wo_f, wup, wconv, wdn)


def _dft_mats(n, inner=64):
    j = jnp.arange(n, dtype=jnp.int32)[:, None]
    ang = lambda k: ((j * k[None, :]) % n).astype(F32) * (2.0 * math.pi / n)
    a = ang(jnp.arange(n // inner, dtype=jnp.int32) * inner)[:, :, None]
    b = ang(jnp.arange(inner, dtype=jnp.int32))[:, None, :]
    s = 1.0 / math.sqrt(n)
    ca, sa, cb, sb = jnp.cos(a) * s, jnp.sin(a) * s, jnp.cos(b), jnp.sin(b)
    return ((ca * cb - sa * sb).reshape(n, n).astype(BF16), (sa * cb + ca * sb).reshape(n, n).astype(BF16))


def _channel_dft():
    k = np.arange(F_GROUP_W)
    ang = 2.0 * np.pi * ((k[:, None] * k[None, :]) % F_GROUP_W) / F_GROUP_W
    eye = np.eye(F_GROUPS)
    s = 1.0 / math.sqrt(F_GROUP_W)
    return np.concatenate([np.kron(eye, np.cos(ang) * s), np.kron(eye, np.sin(ang) * s)], axis=1).astype(np.float32)


def kernel(x_prompt, x_sample, c, cache_mla_ckv, cache_mla_krope, cache_diff_k, cache_diff_v, c_ctx, w_ada, b_ada,
           g_mix_norm, g_ffn_norm, w_in, g_kv_norm, w_uk, w_uv, lam_q1, lam_k1, lam_q2, lam_k2, g_diff_subln,
           w_out, w_up, w_conv, w_down, g_final):
    batch_c, seq_c, _ = x_prompt.shape
    batch_l, seq_l, _ = x_sample.shape
    depth = w_in.shape[0]
    assert seq_c % TM == 0 and seq_l % TM == 0 and seq_l % GRID_W == 0 and D_FF % FF_CHUNK == 0
    assert seq_c % TQ_CTX == 0 and seq_l % TQ_LAT == 0 and batch_c % BB_CTX == 0 and seq_l % TM_IN_LAT == 0
    assert seq_l % TM_LAT == 0 and (batch_c * seq_c) % TM_IN_CTX == 0 and HEADS % HEADS_LAT == 0

    w1 = _take(w_in.astype(BF16), _W1_COLS)
    w2 = _take(jnp.concatenate([w_uk, w_uv], axis=-1).astype(BF16), _W2_COLS)
    wo_a = _take(w_out.astype(BF16), _WOUT_ROWS, axis=1)
    wo_f = w_out[:, HEADS * (MLA_DV + DIFF_DV):, :].astype(BF16)
    wup, wconv, wdn = w_up.astype(BF16), w_conv, w_down.astype(BF16)
    vec = lambda a: a.reshape(depth, 1, a.shape[-1])
    g_mix, g_ffn, g_kv = vec(g_mix_norm), vec(g_ffn_norm), vec(g_kv_norm)
    gsub = jnp.concatenate([jnp.ones_like(g_diff_subln), g_diff_subln], axis=-1).reshape(depth, 1, LANE)
    lam_init = np.asarray([0.8 - 0.6 * math.exp(-0.3 * l) for l in range(depth)], np.float32)
    lam_init_rows = jnp.asarray(np.broadcast_to(lam_init[:, None, None], (depth, 1, LANE)))
    sub_rows = jnp.asarray(np.broadcast_to((1.0 - lam_init)[:, None, None], (depth, 1, LANE)).astype(np.float32))
    g_fin = g_final.reshape(1, D_MODEL)
    cs64 = jnp.asarray(_channel_dft()).astype(BF16)
    tables = tuple(jnp.asarray(a) for a in _rope_tables(seq_l))
    dft_c, dft_l = _dft_mats(seq_c), _dft_mats(seq_l)

    n_rows = -(-(batch_l + 1) // 8) * 8
    cond = jnp.zeros((n_rows, D_MODEL), F32).at[:batch_l].set(c).at[batch_l].set(c_ctx)
    mod, lam = _modulation_call(cond, w_ada, b_ada, lam_q1, lam_k1, lam_q2, lam_k2, lam_init_rows)
    mod5 = mod.reshape(depth, n_rows, w_ada.shape[2] // D_MODEL, 1, D_MODEL)
    row_l = lambda i, tiles_per_seq: i // tiles_per_seq
    row_c = lambda i, tiles_per_seq: batch_l

    flat = lambda a: a.reshape(a.shape[0], a.shape[1], a.shape[2], -1)
    past_kv = _cacheprep_call(cache_mla_ckv, cache_mla_krope, flat(cache_diff_k), flat(cache_diff_v), w2,
                              jnp.asarray(_E_KR).astype(BF16), jnp.asarray(_E_UP).astype(BF16))

    x_c = x_prompt.reshape(batch_c * seq_c, D_MODEL)
    x_l = x_sample.reshape(batch_l * seq_l, D_MODEL)
    caches = tuple(jnp.zeros((batch_c, depth, seq_c, w), F32)
                   for w in (MLA_RANK, MLA_ROPE, HEADS * 2 * DIFF_DH, HEADS * DIFF_DV))
    for l in range(depth):
        l_arr = jnp.full((1,), l, jnp.int32)
        final = l == depth - 1
        qm, km, qd, kd, vm, vd, ab, *caches = _inproj_call(
            l_arr, x_c, mod5, row_c, g_mix, w1, g_kv, w2, cs64, None, seq_c, caches, TM_IN_CTX)
        k3 = lambda a: a.reshape(batch_c, seq_c, HW)
        o = _attn_call(l_arr, lam, gsub, sub_rows, qm, qd, [(k3(km), k3(kd), k3(vm), k3(vd))],
                       batch_c, seq_c, TQ_CTX, HEADS, BB_CTX)
        y = _dft_call(*dft_c, ab, batch_c, seq_c)
        x_c = _mixffn_call(l_arr, o, y, x_c, mod5, row_c, g_ffn, g_fin, wo_a, wo_f, wup, wconv, wdn, seq_c, final,
                           TM_CTX, True)
        qm, km, qd, kd, vm, vd, ab = _inproj_call(
            l_arr, x_l, mod5, row_l, g_mix, w1, g_kv, w2, cs64, tables, seq_l, None, TM_IN_LAT)
        k3 = lambda a: a.reshape(batch_l, seq_l, HW)
        o = _attn_call(l_arr, lam, gsub, sub_rows, qm, qd, [(k3(km), k3(kd), k3(vm), k3(vd)), past_kv],
                       batch_l, seq_l, TQ_LAT, HEADS_LAT, 1)
        y = _dft_call(*dft_l, ab, batch_l, seq_l)
        x_l = _mixffn_call(l_arr, o, y, x_l, mod5, row_l, g_ffn, g_fin, wo_a, wo_f, wup, wconv, wdn, seq_l, final,
                           TM_LAT, False)

    new_ckv, new_kr, new_kd, new_vd = caches
    return (x_c.reshape(batch_c, seq_c, D_MODEL),
            x_l.reshape(batch_l, seq_l, D_MODEL),
            new_ckv, new_kr,
            new_kd.reshape(batch_c, depth, seq_c, HEADS, 2 * DIFF_DH),
            new_vd.reshape(batch_c, depth, seq_c, HEADS, DIFF_DV))
```

```python
import functools
import math

import numpy as np
import jax
import jax.numpy as jnp
from jax import lax
from jax.experimental import pallas as pl
from jax.experimental.pallas import tpu as pltpu

F32 = jnp.float32
BF16 = jnp.bfloat16

D_MODEL = 1024
GRID_W = 64
HEADS = 6
MLA_NOPE, MLA_ROPE, MLA_DV, MLA_RANK = 64, 32, 64, 128
DIFF_DH, DIFF_DV = 32, 64
F_GROUPS, F_GROUP_W = 4, 64
F_WIDTH = F_GROUPS * F_GROUP_W
D_FF = 2816
ROPE_BASE = 10000.0
EPS = 1e-6
MLA_SCALE = (MLA_NOPE + MLA_ROPE) ** -0.5
DIFF_SCALE = DIFF_DH ** -0.5
LOG2E = math.log2(math.e)

_QM0 = 0
_CKV0 = HEADS * (MLA_NOPE + MLA_ROPE)
_KR0 = _CKV0 + MLA_RANK
_QD0 = _KR0 + MLA_ROPE
_KD0 = _QD0 + HEADS * 2 * DIFF_DH
_VD0 = _KD0 + HEADS * 2 * DIFF_DH
_UF0 = _VD0 + HEADS * DIFF_DV

LANE = 128
HW = HEADS * LANE
TM = 256
TM_CTX = 512
TM_LAT = 512
TM_IN_LAT = 1024
TM_IN_CTX = 512
TQ_CTX = 256
TQ_LAT = 1024
HEADS_LAT = 2
TQ_SUB = 512
BB_CTX = 4
DFT_ROWS_PER_STEP = 8192
PHASED_SCORE_BYTES = 32 << 20
FF_CHUNK = 256
N_FF_CHUNKS = D_FF // FF_CHUNK
HALO = 8
VMEM_LIMIT = 56 * 1024 * 1024


def _cparams(sem):
    return pltpu.CompilerParams(dimension_semantics=sem, vmem_limit_bytes=VMEM_LIMIT)


def _w1_columns():
    z = lambda n: [-1] * n
    r = lambda a, n: list(range(a, a + n))
    cols = []
    cols += r(_CKV0, MLA_RANK)
    cols += z(MLA_NOPE) + r(_KR0, MLA_ROPE) + z(LANE - MLA_NOPE - MLA_ROPE)
    for h in range(HEADS):
        b = _QM0 + h * (MLA_NOPE + MLA_ROPE)
        cols += r(b, MLA_NOPE) + r(b + MLA_NOPE, MLA_ROPE) + r(b + MLA_NOPE, MLA_ROPE)
    cols += r(_QD0, _UF0 + F_WIDTH - _QD0)
    cols += r(_KR0, MLA_ROPE) + z(LANE - MLA_ROPE)
    return np.asarray(cols, np.int32)


_W1_COLS = _w1_columns()
DIFF_W = HEADS * 2 * DIFF_DH
assert DIFF_W == HEADS * DIFF_DV and 2 * DIFF_DH == LANE // 2
_C_CKV, _C_KR, _C_QM = 0, LANE, 2 * LANE
_C_QD = _C_QM + HW
_C_KD = _C_QD + DIFF_W
_C_VD = _C_KD + DIFF_W
_C_UF = _C_VD + DIFF_W
_C_XKR = _C_UF + F_WIDTH


def _w2_columns():
    cols = []
    for base in (0, HEADS * MLA_NOPE):
        for h in range(HEADS):
            cols += list(range(base + h * 64, base + (h + 1) * 64)) + [-1] * 64
    return np.asarray(cols, np.int32)


_W2_COLS = _w2_columns()


def _wout_rows():
    rows = []
    for h in range(HEADS):
        rows += list(range(h * MLA_DV, (h + 1) * MLA_DV))
        rows += list(range(HEADS * MLA_DV + h * DIFF_DV, HEADS * MLA_DV + (h + 1) * DIFF_DV))
    return np.asarray(rows, np.int32)


_WOUT_ROWS = _wout_rows()


def _take(w, idx, axis=-1):
    axis = axis % w.ndim
    pieces, i, n = [], 0, len(idx)
    while i < n:
        j = i + 1
        if idx[i] < 0:
            while j < n and idx[j] < 0:
                j += 1
            shape = w.shape[:axis] + (j - i,) + w.shape[axis + 1:]
            pieces.append(jnp.zeros(shape, w.dtype))
        else:
            while j < n and idx[j] == idx[j - 1] + 1:
                j += 1
            pieces.append(lax.slice_in_dim(w, int(idx[i]), int(idx[i]) + j - i, axis=axis))
        i = j
    return jnp.concatenate(pieces, axis=axis)


def _selection(src_of_col, n_src):
    e = np.zeros((n_src, len(src_of_col)), np.float32)
    for c, s in enumerate(src_of_col):
        if s >= 0:
            e[s, c] = 1.0
    return e


def _cache_selections():
    kr, up = [], []
    for h in range(HEADS):
        kr += [-1] * (MLA_NOPE + MLA_ROPE) + list(range(MLA_ROPE))
        up += [-1] * (LANE // 2) + list(range(h * LANE // 2, (h + 1) * LANE // 2))
    return _selection(kr, MLA_ROPE), _selection(up, DIFF_W)


_E_KR, _E_UP = _cache_selections()


def _rope_tables(n_lat):
    t = np.arange(n_lat)
    row = (t // GRID_W).astype(np.float64)
    col = (t % GRID_W).astype(np.float64)
    quarter = MLA_ROPE // 4
    inv = ROPE_BASE ** (-np.arange(quarter, dtype=np.float64) / quarter)
    inv = inv.astype(np.float32).astype(np.float64)
    ang = np.concatenate([row[:, None] * inv, row[:, None] * inv, col[:, None] * inv, col[:, None] * inv], axis=1)
    ang = ang.astype(np.float32)
    cos32 = np.cos(ang).astype(np.float32)
    sign = np.concatenate([-np.ones(quarter), np.ones(quarter)] * 2).astype(np.float32)
    sin32 = np.sin(ang).astype(np.float32) * sign
    one, zero = np.ones((n_lat, 32), np.float32), np.zeros((n_lat, 32), np.float32)
    cm = np.concatenate([one, one, cos32, one], axis=1)
    sm = np.concatenate([zero, zero, sin32, zero], axis=1)
    cd = np.concatenate([cos32] * 4, axis=1)
    sd = np.concatenate([sin32] * 4, axis=1)
    return cm, sm, cd, sd


def _mod_kernel(cond_ref, w_ref, b_ref, q1_ref, k1_ref, q2_ref, k2_ref, li_ref, mod_ref, lam_ref):
    a = cond_ref[...]
    a = a * jax.nn.sigmoid(a)
    acc = jnp.dot(a.astype(BF16), w_ref[...].astype(BF16), preferred_element_type=F32)
    mod_ref[...] = acc + b_ref[...]
    d1 = jnp.sum(q1_ref[...] * k1_ref[...], axis=-1, keepdims=True)
    d2 = jnp.sum(q2_ref[...] * k2_ref[...], axis=-1, keepdims=True)
    lam_ref[...] = jnp.exp(d1) - jnp.exp(d2) + li_ref[...]


def _modulation_call(cond, w_ada, b_ada, lam_q1, lam_k1, lam_q2, lam_k2, lam_init_rows):
    depth = w_ada.shape[0]
    n_sec = w_ada.shape[2] // D_MODEL
    rows = cond.shape[0]
    vec = lambda a: a.reshape(depth, 1, a.shape[-1])
    lam_spec = pl.BlockSpec((None, 1, DIFF_DH), lambda l, s: (l, 0, 0))
    return pl.pallas_call(
        _mod_kernel,
        grid=(depth, n_sec),
        in_specs=[
            pl.BlockSpec((rows, D_MODEL), lambda l, s: (0, 0)),
            pl.BlockSpec((None, D_MODEL, D_MODEL), lambda l, s: (l, 0, s)),
            pl.BlockSpec((None, 1, D_MODEL), lambda l, s: (l, 0, s)),
            lam_spec, lam_spec, lam_spec, lam_spec,
            pl.BlockSpec((None, 1, LANE), lambda l, s: (l, 0, 0)),
        ],
        out_specs=[
            pl.BlockSpec((None, rows, D_MODEL), lambda l, s: (l, 0, s)),
            pl.BlockSpec((None, 1, LANE), lambda l, s: (l, 0, 0)),
        ],
        out_shape=[
            jax.ShapeDtypeStruct((depth, rows, n_sec * D_MODEL), F32),
            jax.ShapeDtypeStruct((depth, 1, LANE), F32),
        ],
        compiler_params=_cparams(("arbitrary", "arbitrary")),
        name="modulation",
    )(cond, w_ada, vec(b_ada), vec(lam_q1), vec(lam_k1), vec(lam_q2), vec(lam_k2), lam_init_rows)


def _rms(x):
    return x * lax.rsqrt(jnp.mean(x * x, axis=-1, keepdims=True) + EPS)


def _swap_halves(x):
    lane = lax.broadcasted_iota(jnp.int32, x.shape, 1)
    first = (lane & 8) == 0
    return jnp.where(first, pltpu.roll(x, LANE - 8, axis=1), pltpu.roll(x, 8, axis=1))


def _rope(x, c, s):
    return x * c + _swap_halves(x) * s


def _store_lane_blocks(ref, x):
    for k in range(ref.shape[0]):
        ref[k] = x[:, k * LANE:(k + 1) * LANE]


def _strided_rows(ref, start_of_group, stride, groups):
    return jnp.concatenate(
        [jnp.concatenate([ref[k, pl.ds(start_of_group(j), 8, stride=stride), :] for k in range(ref.shape[0])], axis=1)
         for j in range(groups)], axis=0)


def _inproj_kernel(*refs, rope, ctx, tiles_per_seq):
    it = iter(refs)
    l_ref = next(it)
    x_ref, shift_ref, scale_ref, g_ref, w1_ref, gkv_ref, w2_ref, cs_ref = (next(it) for _ in range(8))
    if rope:
        cm_ref, sm_ref, cd_ref, sd_ref = (next(it) for _ in range(4))
    if ctx:
        for _ in range(4):
            next(it)
    qm_ref, km_ref, qd_ref, kd_ref, vm_ref, vd_ref, ab_ref = (next(it) for _ in range(7))
    if ctx:
        ckv_o, kr_o, kdc_o, vdc_o = (next(it) for _ in range(4))
    h_scr = next(it)
    del l_ref
    tm = x_ref.shape[0]
    upper = lax.broadcasted_iota(jnp.int32, (tm, LANE), 1) >= LANE // 2
    ones_hi = upper.astype(F32)

    x = x_ref[...]
    h = _rms(x) * g_ref[...] * (1.0 + scale_ref[...]) + shift_ref[...]
    h_scr[...] = h.astype(BF16)

    if rope:
        r0 = pl.multiple_of((pl.program_id(0) % tiles_per_seq) * tm, tm)
        cm, sm = cm_ref[pl.ds(r0, tm), :], sm_ref[pl.ds(r0, tm), :]
        cd, sd = cd_ref[pl.ds(r0, tm), :], sd_ref[pl.ds(r0, tm), :]

    def proj(c0, width):
        return jnp.dot(h_scr[...], w1_ref[:, c0:c0 + width], preferred_element_type=F32)

    acc = proj(_C_CKV, 2 * LANE)
    ckvn = _rms(acc[:, :LANE]) * gkv_ref[...]
    kr = acc[:, LANE:]
    def to_cache(ref, val):
        rows = ref.shape[1]
        for s in range(ref.shape[0]):
            ref[s] = val[s * rows:(s + 1) * rows]

    if ctx:
        to_cache(ckv_o, ckvn)
    if rope:
        kr = _rope(kr, cm, sm)
    kv2 = jnp.dot(ckvn.astype(BF16), w2_ref[...], preferred_element_type=F32)
    for hd in range(HEADS):
        blk = slice(hd * LANE, (hd + 1) * LANE)
        km_ref[:, blk] = (kv2[:, blk] + kr).astype(BF16)
        vm_ref[:, blk] = (kv2[:, HW + hd * LANE:HW + (hd + 1) * LANE] + ones_hi).astype(BF16)

    for j in range(HW // (2 * LANE)):
        acc = proj(_C_QM + j * 2 * LANE, 2 * LANE)
        for k in range(2):
            blk = acc[:, k * LANE:(k + 1) * LANE]
            if rope:
                blk = _rope(blk, cm, sm)
            col = (2 * j + k) * LANE
            qm_ref[:, col:col + LANE] = (blk * (MLA_SCALE * LOG2E)).astype(BF16)

    qd_c, kd_c, vd_c = proj(_C_QD, DIFF_W), proj(_C_KD, DIFF_W), proj(_C_VD, DIFF_W)
    if ctx:
        to_cache(kdc_o, kd_c)
        to_cache(vdc_o, vd_c)
    swap = lambda a: pltpu.roll(a, LANE // 2, axis=1)
    for j in range(DIFF_W // LANE):
        pair = slice(j * LANE, (j + 1) * LANE)
        blk_a, blk_b = slice(2 * j * LANE, (2 * j + 1) * LANE), slice((2 * j + 1) * LANE, (2 * j + 2) * LANE)
        q = qd_c[:, pair] * (DIFF_SCALE * LOG2E)
        k = kd_c[:, pair]
        v = vd_c[:, pair]
        q_rot, k_rot = (_rope(q, cd, sd), _rope(k, cd, sd)) if rope else (q, k)
        qd_ref[:, blk_a] = jnp.where(upper, swap(q), q_rot).astype(BF16)
        qd_ref[:, blk_b] = jnp.where(upper, q, swap(q_rot)).astype(BF16)
        kd_ref[:, blk_a] = jnp.where(upper, 0.0, k_rot).astype(BF16)
        kd_ref[:, blk_b] = jnp.where(upper, 0.0, swap(k_rot)).astype(BF16)
        vd_ref[:, blk_a] = jnp.where(upper, swap(v), 1.0).astype(BF16)
        vd_ref[:, blk_b] = jnp.where(upper, v, 1.0).astype(BF16)

    uf = proj(_C_UF, F_WIDTH)
    ab_ref[...] = jnp.dot(uf.astype(BF16), cs_ref[...], preferred_element_type=F32).astype(BF16)

    if ctx:
        to_cache(kr_o, proj(_C_XKR, LANE)[:, :MLA_ROPE])


def _inproj_call(l_arr, x, mod5, b_row, g_mix, w1, g_kv, w2, cs64, tables, seq_len, caches, tm):
    t = x.shape[0]
    n_tiles = t // tm
    tiles_per_seq = max(1, seq_len // tm)
    seqs_per_tile, rows = max(1, tm // seq_len), min(tm, seq_len)
    assert tm % rows == 0 and seq_len % rows == 0
    rope = tables is not None
    ctx = caches is not None
    assert not rope or seqs_per_tile == 1
    n1 = _C_XKR + LANE if ctx else _C_XKR
    row = lambda w: pl.BlockSpec((tm, w), lambda i, l: (i, 0))
    modspec = lambda sec: pl.BlockSpec((None, None, None, 1, D_MODEL),
                                       lambda i, l: (l[0], b_row(i, tiles_per_seq), sec, 0, 0))
    in_specs = [
        row(D_MODEL), modspec(0), modspec(1),
        pl.BlockSpec((None, 1, D_MODEL), lambda i, l: (l[0], 0, 0)),
        pl.BlockSpec((None, D_MODEL, n1), lambda i, l: (l[0], 0, 0)),
        pl.BlockSpec((None, 1, MLA_RANK), lambda i, l: (l[0], 0, 0)),
        pl.BlockSpec((None, MLA_RANK, 2 * HW), lambda i, l: (l[0], 0, 0)),
        pl.BlockSpec((F_WIDTH, 2 * F_WIDTH), lambda i, l: (0, 0)),
    ]
    args = [x, mod5, mod5, g_mix, w1, g_kv, w2, cs64]
    if rope:
        in_specs += [pl.BlockSpec((seq_len, LANE), lambda i, l: (0, 0))] * 4
        args += list(tables)
    out_specs = [row(HW)] * 6 + [row(2 * F_WIDTH)]
    out_shape = [jax.ShapeDtypeStruct((t, HW), BF16)] * 6 + [jax.ShapeDtypeStruct((t, 2 * F_WIDTH), BF16)]
    aliases = {}
    if ctx:
        for a in caches:
            aliases[1 + len(args)] = len(out_shape)
            in_specs.append(pl.BlockSpec(memory_space=pl.ANY))
            args.append(a)
            out_specs.append(pl.BlockSpec((seqs_per_tile, None, rows, a.shape[-1]),
                                          lambda i, l: (i // tiles_per_seq, l[0], i % tiles_per_seq, 0)))
            out_shape.append(jax.ShapeDtypeStruct(a.shape, a.dtype))
    return pl.pallas_call(
        functools.partial(_inproj_kernel, rope=rope, ctx=ctx, tiles_per_seq=tiles_per_seq),
        grid_spec=pltpu.PrefetchScalarGridSpec(
            num_scalar_prefetch=1, grid=(n_tiles,), in_specs=in_specs, out_specs=out_specs,
            scratch_shapes=[pltpu.VMEM((tm, D_MODEL), BF16)]),
        out_shape=out_shape,
        input_output_aliases=aliases,
        compiler_params=_cparams(("arbitrary",)),
        name="inproj_ctx" if ctx else "inproj_lat",
    )(l_arr, *args)


def _cacheprep_kernel(ckv_ref, kr_ref, kd_ref, vd_ref, w2_ref, ekr_ref, eup_ref,
                      km_ref, kdo_ref, vm_ref, vdo_ref):
    dot = lambda a, b: jnp.dot(a.astype(BF16), b, preferred_element_type=F32)
    kv2 = dot(ckv_ref[...], w2_ref[...])
    ones_hi = ((lax.broadcasted_iota(jnp.int32, (1, HW), 1) % LANE) >= LANE // 2).astype(F32)
    km_ref[...] = (kv2[:, :HW] + dot(kr_ref[...], ekr_ref[...])).astype(BF16)
    kdo_ref[...] = dot(kd_ref[...], eup_ref[...]).astype(BF16)
    vm_ref[...] = (kv2[:, HW:] + ones_hi).astype(BF16)
    vdo_ref[...] = (dot(vd_ref[...], eup_ref[...]) + (1.0 - ones_hi)).astype(BF16)


def _cacheprep_call(cache_ckv, cache_kr, cache_kd, cache_vd, w2, ekr, eup):
    b, depth, p = cache_ckv.shape[:3]
    cspec = lambda w: pl.BlockSpec((None, None, p, w), lambda l, i: (i, l, 0, 0))
    const = lambda a: pl.BlockSpec(a.shape, lambda l, i: (0,) * a.ndim)
    ospec = pl.BlockSpec((None, None, p, HW), lambda l, i: (l, i, 0, 0))
    oshape = jax.ShapeDtypeStruct((depth, b, p, HW), BF16)
    return pl.pallas_call(
        _cacheprep_kernel,
        grid=(depth, b),
        in_specs=[cspec(MLA_RANK), cspec(MLA_ROPE), cspec(HEADS * 2 * DIFF_DH), cspec(HEADS * DIFF_DV),
                  pl.BlockSpec((None, MLA_RANK, 2 * HW), lambda l, i: (l, 0, 0)),
                  const(ekr), const(eup)],
        out_specs=[ospec] * 4,
        out_shape=[oshape] * 4,
        compiler_params=_cparams(("arbitrary", "arbitrary")),
        name="cacheprep",
    )(cache_ckv, cache_kr, cache_kd, cache_vd, w2, ekr, eup)


def _attn_kernel(*refs, n_seg, heads, tq, group):
    it = iter(refs)
    l_ref = next(it)
    lam_ref, gsub_ref, sub_ref, qm_ref, qd_ref = (next(it) for _ in range(5))
    km = [next(it) for _ in range(n_seg)]
    kd = [next(it) for _ in range(n_seg)]
    vm = [next(it) for _ in range(n_seg)]
    vd = [next(it) for _ in range(n_seg)]
    o_ref = next(it)
    del l_ref

    sub = min(tq, TQ_SUB)
    lane = lax.broadcasted_iota(jnp.int32, (sub, LANE), 1)
    upper = lane >= LANE // 2
    first = (lane // DIFF_DH) % 2 == 0
    nt = (((1,), (1,)), ((), ()))
    problems = [(bi, slice(hd * LANE, (hd + 1) * LANE), slice(r * sub, (r + 1) * sub))
                for bi in range(qm_ref.shape[0]) for hd in range(heads) for r in range(tq // sub)]
    at = lambda refs, bi: [r.at[bi] for r in refs]

    outs = []
    for g0 in range(0, len(problems), group):
        qs, ks, vs, lanes = [], [], [], []
        for bi, blk, rows in problems[g0:g0 + group]:
            qd = qd_ref[bi, rows, blk]
            zero = jnp.zeros_like(qd)
            qs += [qm_ref[bi, rows, blk], jnp.where(first, qd, zero), jnp.where(first, zero, qd)]
            ks += [at(km, bi), at(kd, bi), at(kd, bi)]
            vs += [at(vm, bi), at(vd, bi), at(vd, bi)]
            lanes += [blk] * 3
        scores = [[lax.dot_general(q, k[:, blk], nt, preferred_element_type=F32) for k in kk]
                  for q, kk, blk in zip(qs, ks, lanes)]
        maxes = [functools.reduce(jnp.maximum, [jnp.max(si, axis=-1, keepdims=True) for si in s]) for s in scores]
        probs = [[jnp.exp2(si - m).astype(BF16) for si in s] for s, m in zip(scores, maxes)]
        raw = [functools.reduce(jnp.add, [jnp.dot(pi, v[:, blk], preferred_element_type=F32)
                                          for pi, v in zip(p, vv)])
               for p, vv, blk in zip(probs, vs, lanes)]
        outs += [o / pltpu.roll(o, LANE // 2, axis=1) for o in raw]

    for n, (bi, blk, rows) in enumerate(problems):
        o_m, o_1, o_2 = outs[3 * n:3 * n + 3]
        o_d = jnp.where(upper, o_1 - lam_ref[...] * o_2, 0.0)
        ms = jnp.sum(o_d * o_d, axis=-1, keepdims=True) * (1.0 / DIFF_DV)
        o_d = o_d * lax.rsqrt(ms + EPS) * gsub_ref[...] * sub_ref[...]
        o_ref[bi, rows, blk] = jnp.where(upper, o_d, o_m).astype(BF16)


def _attn_call(l_arr, lam, gsub, sub, qm, qd, segs, batch, n_q, tq, heads, bb):
    n_seg = len(segs)
    width = heads * LANE
    q3 = lambda a: a.reshape(batch, n_q, HW)
    layer_vec = pl.BlockSpec((None, 1, LANE), lambda b, h, i, l: (l[0], 0, 0))
    qspec = pl.BlockSpec((bb, tq, width), lambda b, h, i, l: (b, i, h))
    kspecs, kargs, n_keys = [], [], 0
    for which in range(4):
        for seg in segs:
            a = seg[which]
            if a.ndim == 4:
                kspecs.append(pl.BlockSpec((None, bb, a.shape[2], width), lambda b, h, i, l: (l[0], b, 0, h)))
            else:
                kspecs.append(pl.BlockSpec((bb, a.shape[1], width), lambda b, h, i, l: (b, 0, h)))
            kargs.append(a)
            n_keys += a.shape[-2] if which == 0 else 0
    return pl.pallas_call(
        functools.partial(_attn_kernel, n_seg=n_seg, heads=heads, tq=tq,
                          group=max(1, PHASED_SCORE_BYTES // (3 * min(tq, TQ_SUB) * n_keys * 4))),
        grid_spec=pltpu.PrefetchScalarGridSpec(
            num_scalar_prefetch=1, grid=(batch // bb, HEADS // heads, n_q // tq),
            in_specs=[layer_vec, layer_vec, layer_vec, qspec, qspec] + kspecs,
            out_specs=qspec),
        out_shape=jax.ShapeDtypeStruct((batch, n_q, HW), BF16),
        compiler_params=_cparams(("arbitrary", "arbitrary", "arbitrary")),
        name="attention_%dseg" % n_seg,
    )(l_arr, lam, gsub, sub, q3(qm), q3(qd), *kargs).reshape(batch * n_q, HW)


def _dft_kernel(c_ref, s_ref, ab_ref, y_ref):
    for bi in range(ab_ref.shape[0]):
        ya = jnp.dot(c_ref[...], ab_ref[bi, :, :F_WIDTH], preferred_element_type=F32)
        yb = jnp.dot(s_ref[...], ab_ref[bi, :, F_WIDTH:], preferred_element_type=F32)
        y_ref[bi] = (ya - yb).astype(BF16)


def _dft_call(cn, sn, ab, batch, n):
    tr = min(n, TM)
    bb = math.gcd(batch, max(1, DFT_ROWS_PER_STEP // n))
    mat = pl.BlockSpec((tr, n), lambda b, r: (r, 0))
    return pl.pallas_call(
        _dft_kernel,
        grid=(batch // bb, n // tr),
        in_specs=[mat, mat, pl.BlockSpec((bb, n, 2 * F_WIDTH), lambda b, r: (b, 0, 0))],
        out_specs=pl.BlockSpec((bb, tr, F_WIDTH), lambda b, r: (b, r, 0)),
        out_shape=jax.ShapeDtypeStruct((batch, n, F_WIDTH), BF16),
        compiler_params=_cparams(("arbitrary", "arbitrary")),
        name="position_dft",
    )(cn, sn, ab.reshape(batch, n, 2 * F_WIDTH)).reshape(batch * n, F_WIDTH)


def _mixffn_kernel(l_ref, o_ref, op_ref, on_ref, y_ref, yp_ref, yn_ref, x_ref, xp_ref, xn_ref,
                   ga_ref, sf_ref, cf_ref, gf_ref, g_ref, gfin_ref, wa_ref, wf_ref, wup_ref, wconv_ref, wdn_ref,
                   xo_ref, lo_ref, ly_ref, hs_ref, hx_ref, act_ref, ys_ref, *, seq_len, final):
    del l_ref
    tm = x_ref.shape[0]
    groups = tm // 8
    tiles_per_seq = max(1, seq_len // tm)
    i = pl.program_id(0)
    has_prev = (i % tiles_per_seq) != 0
    has_next = (i % tiles_per_seq) != tiles_per_seq - 1
    row8 = lax.broadcasted_iota(jnp.int32, (8, FF_CHUNK), 0)
    opens = [s for s in range(1, 8) if (s * groups) % seq_len == 0]
    closes = [s for s in range(0, 7) if ((s + 1) * groups) % seq_len == 0]
    keep_dn = functools.reduce(jnp.logical_and, [row8 != s for s in opens], row8 >= 0)
    keep_up = functools.reduce(jnp.logical_and, [row8 != s for s in closes], row8 >= 0)
    for dst, parts in ((lo_ref, (o_ref, op_ref, on_ref)), (ly_ref, (y_ref, yp_ref, yn_ref))):
        dst[0:tm, :] = parts[0][...]
        dst[tm:tm + HALO, :] = parts[1][...]
        dst[tm + HALO:, :] = parts[2][...]
    attn = jnp.dot(lo_ref[...], wa_ref[...], preferred_element_type=F32)
    attn = attn + jnp.dot(ly_ref[...], wf_ref[...], preferred_element_type=F32)
    x = jnp.concatenate([x_ref[...], xp_ref[...], xn_ref[...]], axis=0) + ga_ref[...] * attn
    xo_ref[...] = x[:tm]
    h = _rms(x) * g_ref[...] * (1.0 + cf_ref[...]) + sf_ref[...]
    mlp_row = lambda j: 8 * ((8 * j) % groups) + (8 * j) // groups
    for j in range(groups):
        for k in range(D_MODEL // LANE):
            hs_ref[k, pl.ds(mlp_row(j), 8, stride=8), :] = h[8 * j:8 * j + 8, k * LANE:(k + 1) * LANE]
    hx_ref[0:tm, :] = jnp.concatenate([hs_ref[k] for k in range(D_MODEL // LANE)], axis=1).astype(BF16)
    zero = jnp.zeros((HALO, D_MODEL), F32)
    hx_ref[tm:tm + HALO, :] = jnp.where(has_prev, h[tm:tm + HALO], zero).astype(BF16)
    hx_ref[tm + HALO:, :] = jnp.where(has_next, h[tm + HALO:], zero).astype(BF16)

    for c in range(N_FF_CHUNKS):
        halves = []
        for cols in (slice(c * FF_CHUNK, (c + 1) * FF_CHUNK),
                     slice(D_FF + c * FF_CHUNK, D_FF + (c + 1) * FF_CHUNK)):
            u = jnp.dot(hx_ref[...], wup_ref[:, cols], preferred_element_type=F32)
            wc = wconv_ref[:, cols]
            mid = u[:tm]
            wrap_dn = jnp.concatenate([u[tm + HALO - 1:tm + HALO], mid[tm - 8:tm - 1]], axis=0)
            wrap_up = jnp.concatenate([mid[1:8], u[tm + HALO:tm + HALO + 1]], axis=0)
            if opens:
                wrap_dn = jnp.where(keep_dn, wrap_dn, 0.0)
            if closes:
                wrap_up = jnp.where(keep_up, wrap_up, 0.0)
            dn = jnp.concatenate([wrap_dn, mid[:tm - 8]], axis=0)
            up = jnp.concatenate([mid[8:], wrap_up], axis=0)
            halves.append(dn * wc[0:1, :] + mid * wc[1:2, :] + up * wc[2:3, :])
        gate, val = halves
        act_ref[:, c * FF_CHUNK:(c + 1) * FF_CHUNK] = (gate * jax.nn.sigmoid(gate) * val).astype(BF16)

    _store_lane_blocks(ys_ref, jnp.dot(act_ref[...], wdn_ref[...], preferred_element_type=F32))
    y = _strided_rows(ys_ref, mlp_row, 8, groups)
    x = xo_ref[...] + gf_ref[...] * y
    if final:
        x = _rms(x) * gfin_ref[...]
    xo_ref[...] = x


def _mixffn_call(l_arr, o, y, x, mod5, b_row, g_ffn, g_final, wo_a, wo_f, wup, wconv, wdn, seq_len, final, tm,
                 shared_cond):
    t = x.shape[0]
    n_tiles = t // tm
    per = tm // HALO
    ext = tm + 2 * HALO

    def with_halos(w):
        return [pl.BlockSpec((tm, w), lambda i, l: (i, 0)),
                pl.BlockSpec((HALO, w), lambda i, l: (jnp.maximum(i * per - 1, 0), 0)),
                pl.BlockSpec((HALO, w), lambda i, l: (jnp.minimum((i + 1) * per, n_tiles * per - 1), 0))]

    assert seq_len % tm == 0 or (tm % seq_len == 0 and shared_cond and seq_len % (tm // 8) == 0)
    tiles_per_seq = max(1, seq_len // tm)
    modspec = lambda sec: pl.BlockSpec((None, None, None, 1, D_MODEL),
                                       lambda i, l: (l[0], b_row(i, tiles_per_seq), sec, 0, 0))
    whole = lambda a: pl.BlockSpec((None,) + a.shape[1:], lambda i, l: (l[0],) + (0,) * (a.ndim - 1))
    return pl.pallas_call(
        functools.partial(_mixffn_kernel, seq_len=seq_len, final=final),
        grid_spec=pltpu.PrefetchScalarGridSpec(
            num_scalar_prefetch=1, grid=(n_tiles,),
            in_specs=with_halos(HW) + with_halos(F_WIDTH) + with_halos(D_MODEL)
            + [modspec(2), modspec(3), modspec(4), modspec(5), whole(g_ffn),
               pl.BlockSpec((1, D_MODEL), lambda i, l: (0, 0)),
               whole(wo_a), whole(wo_f), whole(wup), whole(wconv), whole(wdn)],
            out_specs=pl.BlockSpec((tm, D_MODEL), lambda i, l: (i, 0)),
            scratch_shapes=[pltpu.VMEM((ext, HW), BF16), pltpu.VMEM((ext, F_WIDTH), BF16),
                            pltpu.VMEM((D_MODEL // LANE, tm, LANE), F32), pltpu.VMEM((ext, D_MODEL), BF16),
                            pltpu.VMEM((tm, D_FF), BF16), pltpu.VMEM((D_MODEL // LANE, tm, LANE), F32)]),
        out_shape=jax.ShapeDtypeStruct((t, D_MODEL), F32),
        compiler_params=_cparams(("arbitrary",)),
        name="mix_ffn_final" if final else "mix_ffn",
    )(l_arr, o, o, o, y, y, y, x, x, x, mod5, mod5, mod5, mod5, g_ffn, g_final, wo_a, wo_f, wup, wconv, wdn)


def _dft_mats(n, inner=64):
    j = jnp.arange(n, dtype=jnp.int32)[:, None]
    ang = lambda k: ((j * k[None, :]) % n).astype(F32) * (2.0 * math.pi / n)
    a = ang(jnp.arange(n // inner, dtype=jnp.int32) * inner)[:, :, None]
    b = ang(jnp.arange(inner, dtype=jnp.int32))[:, None, :]
    s = 1.0 / math.sqrt(n)
    ca, sa, cb, sb = jnp.cos(a) * s, jnp.sin(a) * s, jnp.cos(b), jnp.sin(b)
    return ((ca * cb - sa * sb).reshape(n, n).astype(BF16), (sa * cb + ca * sb).reshape(n, n).astype(BF16))


def _channel_dft():
    k = np.arange(F_GROUP_W)
    ang = 2.0 * np.pi * ((k[:, None] * k[None, :]) % F_GROUP_W) / F_GROUP_W
    eye = np.eye(F_GROUPS)
    s = 1.0 / math.sqrt(F_GROUP_W)
    return np.concatenate([np.kron(eye, np.cos(ang) * s), np.kron(eye, np.sin(ang) * s)], axis=1).astype(np.float32)


def kernel(x_prompt, x_sample, c, cache_mla_ckv, cache_mla_krope, cache_diff_k, cache_diff_v, c_ctx, w_ada, b_ada,
           g_mix_norm, g_ffn_norm, w_in, g_kv_norm, w_uk, w_uv, lam_q1, lam_k1, lam_q2, lam_k2, g_diff_subln,
           w_out, w_up, w_conv, w_down, g_final):
    batch_c, seq_c, _ = x_prompt.shape
    batch_l, seq_l, _ = x_sample.shape
    depth = w_in.shape[0]
    assert seq_c % TM == 0 and seq_l % TM == 0 and seq_l % GRID_W == 0 and D_FF % FF_CHUNK == 0
    assert seq_c % TQ_CTX == 0 and seq_l % TQ_LAT == 0 and batch_c % BB_CTX == 0 and seq_l % TM_IN_LAT == 0
    assert seq_l % TM_LAT == 0 and (batch_c * seq_c) % TM_IN_CTX == 0 and HEADS % HEADS_LAT == 0

    w1 = _take(w_in.astype(BF16), _W1_COLS)
    w2 = _take(jnp.concatenate([w_uk, w_uv], axis=-1).astype(BF16), _W2_COLS)
    wo_a = _take(w_out.astype(BF16), _WOUT_ROWS, axis=1)
    wo_f = w_out[:, HEADS * (MLA_DV + DIFF_DV):, :].astype(BF16)
    wup, wconv, wdn = w_up.astype(BF16), w_conv, w_down.astype(BF16)
    vec = lambda a: a.reshape(depth, 1, a.shape[-1])
    g_mix, g_ffn, g_kv = vec(g_mix_norm), vec(g_ffn_norm), vec(g_kv_norm)
    gsub = jnp.concatenate([jnp.ones_like(g_diff_subln), g_diff_subln], axis=-1).reshape(depth, 1, LANE)
    lam_init = np.asarray([0.8 - 0.6 * math.exp(-0.3 * l) for l in range(depth)], np.float32)
    lam_init_rows = jnp.asarray(np.broadcast_to(lam_init[:, None, None], (depth, 1, LANE)))
    sub_rows = jnp.asarray(np.broadcast_to((1.0 - lam_init)[:, None, None], (depth, 1, LANE)).astype(np.float32))
    g_fin = g_final.reshape(1, D_MODEL)
    cs64 = jnp.asarray(_channel_dft()).astype(BF16)
    tables = tuple(jnp.asarray(a) for a in _rope_tables(seq_l))
    dft_c, dft_l = _dft_mats(seq_c), _dft_mats(seq_l)

    n_rows = -(-(batch_l + 1) // 8) * 8
    cond = jnp.zeros((n_rows, D_MODEL), F32).at[:batch_l].set(c).at[batch_l].set(c_ctx)
    mod, lam = _modulation_call(cond, w_ada, b_ada, lam_q1, lam_k1, lam_q2, lam_k2, lam_init_rows)
    mod5 = mod.reshape(depth, n_rows, w_ada.shape[2] // D_MODEL, 1, D_MODEL)
    row_l = lambda i, tiles_per_seq: i // tiles_per_seq
    row_c = lambda i, tiles_per_seq: batch_l

    flat = lambda a: a.reshape(a.shape[0], a.shape[1], a.shape[2], -1)
    past_kv = _cacheprep_call(cache_mla_ckv, cache_mla_krope, flat(cache_diff_k), flat(cache_diff_v), w2,
                              jnp.asarray(_E_KR).astype(BF16), jnp.asarray(_E_UP).astype(BF16))

    x_c = x_prompt.reshape(batch_c * seq_c, D_MODEL)
    x_l = x_sample.reshape(batch_l * seq_l, D_MODEL)
    caches = tuple(jnp.zeros((batch_c, depth, seq_c, w), F32)
                   for w in (MLA_RANK, MLA_ROPE, HEADS * 2 * DIFF_DH, HEADS * DIFF_DV))
    for l in range(depth):
        l_arr = jnp.full((1,), l, jnp.int32)
        final = l == depth - 1
        qm, km, qd, kd, vm, vd, ab, *caches = _inproj_call(
            l_arr, x_c, mod5, row_c, g_mix, w1, g_kv, w2, cs64, None, seq_c, caches, TM_IN_CTX)
        k3 = lambda a: a.reshape(batch_c, seq_c, HW)
        o = _attn_call(l_arr, lam, gsub, sub_rows, qm, qd, [(k3(km), k3(kd), k3(vm), k3(vd))],
                       batch_c, seq_c, TQ_CTX, HEADS, BB_CTX)
        y = _dft_call(*dft_c, ab, batch_c, seq_c)
        x_c = _mixffn_call(l_arr, o, y, x_c, mod5, row_c, g_ffn, g_fin, wo_a, wo_f, wup, wconv, wdn, seq_c, final,
                           TM_CTX, True)
        qm, km, qd, kd, vm, vd, ab = _inproj_call(
            l_arr, x_l, mod5, row_l, g_mix, w1, g_kv, w2, cs64, tables, seq_l, None, TM_IN_LAT)
        k3 = lambda a: a.reshape(batch_l, seq_l, HW)
        o = _attn_call(l_arr, lam, gsub, sub_rows, qm, qd, [(k3(km), k3(kd), k3(vm), k3(vd)), past_kv],
                       batch_l, seq_l, TQ_LAT, HEADS_LAT, 1)
        y = _dft_call(*dft_l, ab, batch_l, seq_l)
        x_l = _mixffn_call(l_arr, o, y, x_l, mod5, row_l, g_ffn, g_fin, wo_a, wo_f, wup, wconv, wdn, seq_l, final,
                           TM_LAT, False)

    new_ckv, new_kr, new_kd, new_vd = caches
    return (x_c.reshape(batch_c, seq_c, D_MODEL),
            x_l.reshape(batch_l, seq_l, D_MODEL),
            new_ckv, new_kr,
            new_kd.reshape(batch_c, depth, seq_c, HEADS, 2 * DIFF_DH),
            new_vd.reshape(batch_c, depth, seq_c, HEADS, DIFF_DV))
```
